```python
import jax, jax.numpy as jnp
from jax import lax
import numpy as np

D_MODEL = 2048
BATCH = 1
SEQ = 8192
DEPTH = 4

CTX_LEN = 256
GRID_W = 64
EPS = 1e-6

POOL_WINDOWS = (2, 4, 8, 16)
POOL_WIDTH = D_MODEL // 2
POOL_GROUP = POOL_WIDTH // len(POOL_WINDOWS)
CONV_WIDTH = D_MODEL // 2
CONV_HEADS = 8
CONV_K = 3
EVEN_IN = POOL_WIDTH + 3 * CONV_WIDTH

MLA_HEADS = 16
QK_NOPE = 128
QK_ROPE = 64
V_DIM = 128
Q_LORA = 512
KV_LORA = 512
ROPE_THETA = 10000.0
Q_BLOCK = 128

D_FF = ((8 * D_MODEL // 3 + 255) // 256) * 256

N_EVEN = (DEPTH + 1) // 2
N_ODD = DEPTH // 2

kernel_name = "hybrid_pool_conv_mla_dit_prefix"


def rmsnorm(x, g):
    xf = x.astype(jnp.float32)
    y = xf * lax.rsqrt(jnp.mean(xf * xf, axis=-1, keepdims=True) + EPS)
    return (y * g.astype(jnp.float32)).astype(x.dtype)


def modulate(h, shift, scale):
    return h * (1 + scale) + shift


def ada_mod(cond, w, b):
    return jnp.split(jax.nn.silu(cond) @ w + b, 6, axis=-1)


def swiglu(h, w_gate, w_up, w_down):
    return (jax.nn.silu(h @ w_gate) * (h @ w_up)) @ w_down


def centred_pool_minus_self(u):
    n = u.shape[1]
    uf = u.astype(jnp.float32)
    cs = jnp.concatenate([jnp.zeros_like(uf[:, :1]), jnp.cumsum(uf, axis=1)], axis=1)
    t = jnp.arange(n)
    outs = []
    for g, w in enumerate(POOL_WINDOWS):
        lo = jnp.clip(t - w // 2, 0, n)
        hi = jnp.clip(t + (w - w // 2), 0, n)
        csg = cs[..., g * POOL_GROUP:(g + 1) * POOL_GROUP]
        s = jnp.take(csg, hi, axis=1) - jnp.take(csg, lo, axis=1)
        cnt = (hi - lo).astype(jnp.float32)[None, :, None]
        outs.append(s / cnt)
    pooled = jnp.concatenate(outs, axis=-1)
    return (pooled - uf).astype(u.dtype)


def pool_conv_mixer(h, w_in, pool_w, pool_scale, conv_w, w_out):
    bsz, n, _ = h.shape
    z = h @ w_in
    u_pool, gate_b, gate_c, v = jnp.split(
        z, [POOL_WIDTH, POOL_WIDTH + CONV_WIDTH, POOL_WIDTH + 2 * CONV_WIDTH], axis=-1)
    p = centred_pool_minus_self(u_pool).reshape(bsz, n, len(POOL_WINDOWS), POOL_GROUP)
    y_a = jnp.einsum('bngc,gcd->bngd', p, pool_w).reshape(bsz, n, POOL_WIDTH) * pool_scale
    u = gate_c * v
    pad = CONV_K // 2
    up = jnp.pad(u, ((0, 0), (pad, pad), (0, 0)))
    conv = sum(up[:, k:k + n] * conv_w[k] for k in range(CONV_K))
    y_b = gate_b * conv
    return jnp.concatenate([y_a, y_b], axis=-1) @ w_out


def axial_tables(n, dtype):
    rows = n // GRID_W
    row = jnp.repeat(jnp.arange(rows), GRID_W).astype(jnp.float32)
    col = jnp.tile(jnp.arange(GRID_W), rows).astype(jnp.float32)
    nf = QK_ROPE // 4
    inv = jnp.power(jnp.float32(ROPE_THETA), -jnp.arange(nf, dtype=jnp.float32) / nf)
    ar = row[:, None] * inv
    ac = col[:, None] * inv
    return (jnp.cos(ar).astype(dtype), jnp.sin(ar).astype(dtype),
            jnp.cos(ac).astype(dtype), jnp.sin(ac).astype(dtype))


def rope_half(t, cos, sin):
    half = t.shape[-1] // 2
    t1, t2 = t[..., :half], t[..., half:]
    return jnp.concatenate([t1 * cos - t2 * sin, t2 * cos + t1 * sin], axis=-1)


def axial_rope(t, cos_r, sin_r, cos_c, sin_c):
    h = QK_ROPE // 2
    return jnp.concatenate([rope_half(t[..., :h], cos_r, sin_r),
                            rope_half(t[..., h:], cos_c, sin_c)], axis=-1)


def mla_query(h, w_dq, q_norm_g, w_uq):
    bsz, n, _ = h.shape
    q = (rmsnorm(h @ w_dq, q_norm_g) @ w_uq).reshape(bsz, n, MLA_HEADS, QK_NOPE + QK_ROPE)
    return q[..., :QK_NOPE], q[..., QK_NOPE:]


def mla_keyvalue(h, w_dkv, kv_norm_g, w_ukv):
    bsz, n, _ = h.shape
    kv_a = h @ w_dkv
    c_kv = rmsnorm(kv_a[..., :KV_LORA], kv_norm_g)
    k_rope = kv_a[..., KV_LORA:]
    kv = (c_kv @ w_ukv).reshape(bsz, n, MLA_HEADS, QK_NOPE + V_DIM)
    return kv[..., :QK_NOPE], k_rope, kv[..., QK_NOPE:]


def mla_attend(q_nope, q_rope, k_nope, k_rope, v):
    scale = (QK_NOPE + QK_ROPE) ** -0.5
    s = (jnp.einsum('bqhd,bkhd->bhqk', q_nope, k_nope)
         + jnp.einsum('bqhr,bkr->bhqk', q_rope, k_rope))
    p = jax.nn.softmax(s.astype(jnp.float32) * scale, axis=-1).astype(v.dtype)
    return jnp.einsum('bhqk,bkhd->bqhd', p, v)


def blocked_attend(q_nope, q_rope, k_nope, k_rope, v):
    bsz, n, nh, _ = q_nope.shape
    nb = n // Q_BLOCK

    def to_blocks(t):
        return jnp.moveaxis(t.reshape(bsz, nb, Q_BLOCK, *t.shape[2:]), 1, 0)

    out = lax.map(lambda qs: mla_attend(qs[0], qs[1], k_nope, k_rope, v),
                  (to_blocks(q_nope), to_blocks(q_rope)))
    return jnp.moveaxis(out, 0, 1).reshape(bsz, n, nh, V_DIM)


def mla_mixer(h, hc, need_ctx_out, w_dq, q_norm_g, w_uq, w_dkv, kv_norm_g, w_ukv, w_o):
    bsz, n, _ = h.shape
    cr, sr, cc, sc = axial_tables(n, h.dtype)
    qn, qr = mla_query(h, w_dq, q_norm_g, w_uq)
    qr = axial_rope(qr, cr[:, None], sr[:, None], cc[:, None], sc[:, None])
    kn, kr, v = mla_keyvalue(h, w_dkv, kv_norm_g, w_ukv)
    kr = axial_rope(kr, cr, sr, cc, sc)
    kn_c, kr_c, v_c = mla_keyvalue(hc, w_dkv, kv_norm_g, w_ukv)
    kn_all = jnp.concatenate([kn, kn_c], axis=1)
    kr_all = jnp.concatenate([kr, kr_c], axis=1)
    v_all = jnp.concatenate([v, v_c], axis=1)
    y = blocked_attend(qn, qr, kn_all, kr_all, v_all).reshape(bsz, n, MLA_HEADS * V_DIM) @ w_o
    yc = None
    if need_ctx_out:
        qn_c, qr_c = mla_query(hc, w_dq, q_norm_g, w_uq)
        lc = hc.shape[1]
        yc = mla_attend(qn_c, qr_c, kn_c, kr_c, v_c).reshape(bsz, lc, MLA_HEADS * V_DIM) @ w_o
    return y, yc


def setup_inputs(seed: int = 0) -> dict:
    key = jax.random.key(seed)
    ks = iter(jax.random.split(key, 32))

    def nrm(shape, fan_in, mult=1.0):
        return jax.random.normal(next(ks), shape, jnp.float32) * (mult * fan_in ** -0.5)

    def gain(shape):
        return 1.0 + 0.02 * jax.random.normal(next(ks), shape, jnp.float32)

    x = jax.random.normal(next(ks), (BATCH, SEQ, D_MODEL), jnp.float32)
    c = jax.random.normal(next(ks), (BATCH, D_MODEL), jnp.float32)
    ctx = jax.random.normal(next(ks), (BATCH, CTX_LEN, D_MODEL), jnp.float32)
    c_ctx = jax.random.normal(next(ks), (D_MODEL,), jnp.float32)
    return {
        'x': x, 'c': c, 'ctx': ctx, 'c_ctx': c_ctx,
        'ada_w': nrm((DEPTH, D_MODEL, 6 * D_MODEL), D_MODEL, 0.5),
        'ada_b': 0.02 * jax.random.normal(next(ks), (DEPTH, 6 * D_MODEL), jnp.float32),
        'norm1_g': gain((DEPTH, D_MODEL)),
        'norm2_g': gain((DEPTH, D_MODEL)),
        'even_w_in': nrm((N_EVEN, D_MODEL, EVEN_IN), D_MODEL),
        'pool_w': nrm((N_EVEN, len(POOL_WINDOWS), POOL_GROUP, POOL_GROUP), POOL_GROUP),
        'pool_scale': gain((N_EVEN, POOL_WIDTH)),
        'conv_w': nrm((N_EVEN, CONV_K, CONV_WIDTH), CONV_K),
        'even_w_out': nrm((N_EVEN, POOL_WIDTH + CONV_WIDTH, D_MODEL), POOL_WIDTH + CONV_WIDTH),
        'mla_w_dq': nrm((N_ODD, D_MODEL, Q_LORA), D_MODEL),
        'mla_q_norm_g': gain((N_ODD, Q_LORA)),
        'mla_w_uq': nrm((N_ODD, Q_LORA, MLA_HEADS * (QK_NOPE + QK_ROPE)), Q_LORA),
        'mla_w_dkv': nrm((N_ODD, D_MODEL, KV_LORA + QK_ROPE), D_MODEL),
        'mla_kv_norm_g': gain((N_ODD, KV_LORA)),
        'mla_w_ukv': nrm((N_ODD, KV_LORA, MLA_HEADS * (QK_NOPE + V_DIM)), KV_LORA),
        'mla_w_o': nrm((N_ODD, MLA_HEADS * V_DIM, D_MODEL), MLA_HEADS * V_DIM),
        'ffn_w_gate': nrm((DEPTH, D_MODEL, D_FF), D_MODEL),
        'ffn_w_up': nrm((DEPTH, D_MODEL, D_FF), D_MODEL),
        'ffn_w_down': nrm((DEPTH, D_FF, D_MODEL), D_FF),
        'final_norm_g': gain((D_MODEL,)),
    }


def reference(x, c, ctx, c_ctx, ada_w, ada_b, norm1_g, norm2_g, even_w_in, pool_w, pool_scale,
              conv_w, even_w_out, mla_w_dq, mla_q_norm_g, mla_w_uq, mla_w_dkv, mla_kv_norm_g,
              mla_w_ukv, mla_w_o, ffn_w_gate, ffn_w_up, ffn_w_down, final_norm_g):
    x_lat, x_ctx = x, ctx
    for layer in range(DEPTH):
        last = layer == DEPTH - 1
        odd = layer % 2 == 1
        i = layer // 2
        sh1, sc1, g1, sh2, sc2, g2 = [p[:, None, :] for p in ada_mod(c, ada_w[layer], ada_b[layer])]
        h = modulate(rmsnorm(x_lat, norm1_g[layer]), sh1, sc1)
        hc = None
        if odd or not last:
            csh1, csc1, cg1, csh2, csc2, cg2 = ada_mod(c_ctx, ada_w[layer], ada_b[layer])
            hc = modulate(rmsnorm(x_ctx, norm1_g[layer]), csh1, csc1)
        if odd:
            y, yc = mla_mixer(h, hc, not last, mla_w_dq[i], mla_q_norm_g[i], mla_w_uq[i],
                              mla_w_dkv[i], mla_kv_norm_g[i], mla_w_ukv[i], mla_w_o[i])
        else:
            y = pool_conv_mixer(h, even_w_in[i], pool_w[i], pool_scale[i], conv_w[i], even_w_out[i])
            yc = None
            if not last:
                yc = pool_conv_mixer(hc, even_w_in[i], pool_w[i], pool_scale[i], conv_w[i], even_w_out[i])
        x_lat = x_lat + g1 * y
        h2 = modulate(rmsnorm(x_lat, norm2_g[layer]), sh2, sc2)
        x_lat = x_lat + g2 * swiglu(h2, ffn_w_gate[layer], ffn_w_up[layer], ffn_w_down[layer])
        if not last:
            x_ctx = x_ctx + cg1 * yc
            h2c = modulate(rmsnorm(x_ctx, norm2_g[layer]), csh2, csc2)
            x_ctx = x_ctx + cg2 * swiglu(h2c, ffn_w_gate[layer], ffn_w_up[layer], ffn_w_down[layer])
    return rmsnorm(x_lat, final_norm_g)
```

```python
import functools

import jax
import jax.numpy as jnp
import numpy as np
from jax import lax
from jax.experimental import pallas as pl
from jax.experimental.pallas import tpu as pltpu

D_MODEL = 2048
DEPTH = 4
GRID_W = 64
EPS = 1e-6
POOL_WINDOWS = (2, 4, 8, 16)
POOL_WIDTH = 1024
POOL_GROUP = 256
CONV_WIDTH = 1024
EVEN_IN = POOL_WIDTH + 3 * CONV_WIDTH
MLA_HEADS = 16
QK_NOPE = 128
QK_ROPE = 64
V_DIM = 128
Q_LORA = 512
KV_LORA = 512
ROPE_THETA = 10000.0
D_FF = 5632

V7X_LANES = 128
V7X_SUBLANES = 8
V7X_MXU_DIM = 256
V7X_VMEM_BYTES = 64 * 1024 * 1024

HEAD_PAD = V7X_MXU_DIM
HALO = V7X_SUBLANES
LAT_ROW, CTX_ROW = 0, 1
BF16 = jnp.bfloat16
F32 = jnp.float32


def _params(semantics, vmem_mb):
    return pltpu.CompilerParams(dimension_semantics=semantics,
                                vmem_limit_bytes=vmem_mb * 1024 * 1024)


def _tile_rows(m):
    return min(m, 1024)


def _dot(a, b):
    return jnp.dot(a, b, preferred_element_type=F32)


def _rms(x, g):
    return x * lax.rsqrt(jnp.mean(x * x, axis=-1, keepdims=True) + EPS) * g


def _mod_spec(layer, chunk, ngrid):
    if ngrid == 1:
        return pl.BlockSpec((1, V7X_SUBLANES, D_MODEL), lambda i: (layer, 0, chunk))
    return pl.BlockSpec((1, V7X_SUBLANES, D_MODEL), lambda i, j: (layer, 0, chunk))


def _gain_spec(layer, width, ngrid):
    if ngrid == 1:
        return pl.BlockSpec((1, 1, width), lambda i: (layer, 0, 0))
    return pl.BlockSpec((1, 1, width), lambda i, j: (layer, 0, 0))


def _ada_kernel(cond_ref, w_ref, b_ref, o_ref):
    c = cond_ref[...]
    s = (c * jax.nn.sigmoid(c)).astype(BF16)
    o_ref[0] = _dot(s, w_ref[0].astype(BF16)) + b_ref[0]


def ada_table(cond8, ada_w, ada_b):
    tn = 1024
    n_out = 6 * D_MODEL
    return pl.pallas_call(
        _ada_kernel,
        grid=(DEPTH, n_out // tn),
        in_specs=[pl.BlockSpec((V7X_SUBLANES, D_MODEL), lambda l, j: (0, 0)),
                  pl.BlockSpec((1, D_MODEL, tn), lambda l, j: (l, 0, j)),
                  pl.BlockSpec((1, 1, tn), lambda l, j: (l, 0, j))],
        out_specs=pl.BlockSpec((1, V7X_SUBLANES, tn), lambda l, j: (l, 0, j)),
        out_shape=jax.ShapeDtypeStruct((DEPTH, V7X_SUBLANES, n_out), F32),
        compiler_params=_params(("parallel", "parallel"), 40),
        name="ada_table",
    )(cond8, ada_w, ada_b.reshape(DEPTH, 1, n_out))


def _normmod_matmul_kernel(row, x_ref, g_ref, sh_ref, sc_ref, w_ref, o_ref, h_s):
    @pl.when(pl.program_id(1) == 0)
    def _():
        y = _rms(x_ref[...], g_ref[0])
        h = y * (1.0 + sc_ref[0, row:row + 1, :]) + sh_ref[0, row:row + 1, :]
        h_s[...] = h.astype(BF16)

    o_ref[...] = _dot(h_s[...], w_ref[...]).astype(o_ref.dtype)


def normmod_matmul(x, gains, mods, layer, row, w, tn, name):
    m = x.shape[0]
    tm = _tile_rows(m)
    n_out = w.shape[1]
    return pl.pallas_call(
        functools.partial(_normmod_matmul_kernel, row),
        grid=(m // tm, n_out // tn),
        in_specs=[pl.BlockSpec((tm, D_MODEL), lambda i, j: (i, 0)),
                  _gain_spec(layer, D_MODEL, 2),
                  _mod_spec(layer, 0, 2),
                  _mod_spec(layer, 1, 2),
                  pl.BlockSpec((D_MODEL, tn), lambda i, j: (0, j))],
        out_specs=pl.BlockSpec((tm, tn), lambda i, j: (i, j)),
        out_shape=jax.ShapeDtypeStruct((m, n_out), F32),
        scratch_shapes=[pltpu.VMEM((tm, D_MODEL), BF16)],
        compiler_params=_params(("parallel", "arbitrary"), 48),
        name=name,
    )(x, gains, mods, mods, w)


def _matmul_residual_kernel(row, a_ref, w_ref, x_ref, gate_ref, o_ref):
    y = _dot(a_ref[...], w_ref[...])
    o_ref[...] = x_ref[...] + gate_ref[0, row:row + 1, :] * y


def matmul_residual(a, w, x, mods, layer, row, name):
    m, k = a.shape
    tm = _tile_rows(m)
    tn = 512
    return pl.pallas_call(
        functools.partial(_matmul_residual_kernel, row),
        grid=(m // tm, D_MODEL // tn),
        in_specs=[pl.BlockSpec((tm, k), lambda i, j: (i, 0)),
                  pl.BlockSpec((k, tn), lambda i, j: (0, j)),
                  pl.BlockSpec((tm, tn), lambda i, j: (i, j)),
                  pl.BlockSpec((1, V7X_SUBLANES, tn), lambda i, j: (layer, 0, 2 * (D_MODEL // tn) + j))],
        out_specs=pl.BlockSpec((tm, tn), lambda i, j: (i, j)),
        out_shape=jax.ShapeDtypeStruct((m, D_MODEL), F32),
        compiler_params=_params(("parallel", "parallel"), 40),
        name=name,
    )(a, w, x, mods)


def _ffn_kernel(row, final_norm, x_ref, g_ref, sh_ref, sc_ref, gate_ref, fg_ref,
                wg_ref, wu_ref, wd_ref, o_ref, h_s):
    f = pl.program_id(1)

    @pl.when(f == 0)
    def _():
        y = _rms(x_ref[...], g_ref[0])
        h = y * (1.0 + sc_ref[0, row:row + 1, :]) + sh_ref[0, row:row + 1, :]
        h_s[...] = h.astype(BF16)

    h = h_s[...]
    g = _dot(h, wg_ref[...])
    u = _dot(h, wu_ref[...])
    a = (g * jax.nn.sigmoid(g) * u).astype(BF16)
    d = _dot(a, wd_ref[...])

    @pl.when(f == 0)
    def _():
        o_ref[...] = d

    @pl.when(f > 0)
    def _():
        o_ref[...] += d

    @pl.when(f == pl.num_programs(1) - 1)
    def _():
        out = x_ref[...] + gate_ref[0, row:row + 1, :] * o_ref[...]
        if final_norm:
            out = _rms(out, fg_ref[...])
        o_ref[...] = out


def ffn_residual(x, gains, mods, layer, row, wg, wu, wd, final_g, final_norm, name):
    m = x.shape[0]
    tm = _tile_rows(m)
    tf = 512
    return pl.pallas_call(
        functools.partial(_ffn_kernel, row, final_norm),
        grid=(m // tm, D_FF // tf),
        in_specs=[pl.BlockSpec((tm, D_MODEL), lambda i, f: (i, 0), pipeline_mode=pl.Buffered(1)),
                  _gain_spec(layer, D_MODEL, 2),
                  _mod_spec(layer, 3, 2),
                  _mod_spec(layer, 4, 2),
                  _mod_spec(layer, 5, 2),
                  pl.BlockSpec((1, D_MODEL), lambda i, f: (0, 0)),
                  pl.BlockSpec((D_MODEL, tf), lambda i, f: (0, f)),
                  pl.BlockSpec((D_MODEL, tf), lambda i, f: (0, f)),
                  pl.BlockSpec((tf, D_MODEL), lambda i, f: (f, 0))],
        out_specs=pl.BlockSpec((tm, D_MODEL), lambda i, f: (i, 0)),
        out_shape=jax.ShapeDtypeStruct((m, D_MODEL), F32),
        scratch_shapes=[pltpu.VMEM((tm, D_MODEL), BF16)],
        compiler_params=_params(("parallel", "arbitrary"), 58),
        name=name,
    )(x, gains, mods, mods, mods, final_g, wg, wu, wd)


def _shift_rows(x, k):
    n = x.shape[0]
    return pltpu.roll(x, (n - k) % n, axis=0)


def _even_mid_kernel(seq_len, zp_ref, z_ref, zn_ref, pw_ref, ps_ref, cw_ref, o_ref):
    i = pl.program_id(0)
    tm = z_ref.shape[0]
    first = i == 0
    last = i == pl.num_programs(0) - 1
    row = i * tm + lax.broadcasted_iota(jnp.int32, (tm, 1), 0)

    def with_halo(lo, hi):
        before = jnp.where(first, 0.0, zp_ref[:, lo:hi])
        after = jnp.where(last, 0.0, zn_ref[:, lo:hi])
        return jnp.concatenate([before, z_ref[:, lo:hi], after], axis=0)

    for g, w in enumerate(POOL_WINDOWS):
        lo_c, hi_c = g * POOL_GROUP, (g + 1) * POOL_GROUP
        u = with_halo(lo_c, hi_c)
        b = u
        s = 1
        while s < w:
            b = b + _shift_rows(b, s)
            s *= 2
        win = _shift_rows(b, -(w // 2))[HALO:HALO + tm]
        cnt = (jnp.minimum(row + (w - w // 2), seq_len) - jnp.maximum(row - w // 2, 0)).astype(F32)
        p = win / cnt - z_ref[:, lo_c:hi_c]
        y = _dot(p.astype(BF16), pw_ref[g]) * ps_ref[:, lo_c:hi_c]
        o_ref[:, lo_c:hi_c] = y.astype(o_ref.dtype)

    c0 = POOL_WIDTH
    gate_b = z_ref[:, c0:c0 + CONV_WIDTH]
    u = with_halo(c0 + CONV_WIDTH, c0 + 2 * CONV_WIDTH) * with_halo(c0 + 2 * CONV_WIDTH, c0 + 3 * CONV_WIDTH)
    conv = (_shift_rows(u, -1) * cw_ref[0:1, :] + u * cw_ref[1:2, :] + _shift_rows(u, 1) * cw_ref[2:3, :])
    y_b = gate_b * conv[HALO:HALO + tm]
    o_ref[:, POOL_WIDTH:] = y_b.astype(o_ref.dtype)


def even_mid(z, pool_w, pool_scale, conv_w, name):
    m = z.shape[0]
    tm = min(m, 512)
    nb = tm // HALO
    last_blk = m // HALO - 1
    return pl.pallas_call(
        functools.partial(_even_mid_kernel, m),
        grid=(m // tm,),
        in_specs=[pl.BlockSpec((HALO, EVEN_IN), lambda i: (jnp.maximum(i * nb - 1, 0), 0)),
                  pl.BlockSpec((tm, EVEN_IN), lambda i: (i, 0)),
                  pl.BlockSpec((HALO, EVEN_IN), lambda i: (jnp.minimum((i + 1) * nb, last_blk), 0)),
                  pl.BlockSpec((len(POOL_WINDOWS), POOL_GROUP, POOL_GROUP), lambda i: (0, 0, 0)),
                  pl.BlockSpec((1, POOL_WIDTH), lambda i: (0, 0)),
                  pl.BlockSpec((3, CONV_WIDTH), lambda i: (0, 0))],
        out_specs=pl.BlockSpec((tm, POOL_WIDTH + CONV_WIDTH), lambda i: (i, 0)),
        out_shape=jax.ShapeDtypeStruct((m, POOL_WIDTH + CONV_WIDTH), BF16),
        compiler_params=_params(("parallel",), 48),
        name=name,
    )(z, z, z, pool_w, pool_scale, conv_w)


def _rope_lanes(t, cos_ref, sin_ref):
    return t * cos_ref[...] + pltpu.roll(t, QK_ROPE, axis=1) * sin_ref[...]


def _q_up_kernel(scale, a_ref, g_ref, w_ref, cos_ref, sin_ref, q_ref, n_s):
    @pl.when(pl.program_id(1) == 0)
    def _():
        n_s[...] = _rms(a_ref[...], g_ref[0]).astype(BF16)

    q = _dot(n_s[...], w_ref[...])
    rope = _rope_lanes(q[:, QK_NOPE:], cos_ref, sin_ref)
    q_ref[0] = (jnp.concatenate([q[:, :QK_NOPE], rope], axis=1) * scale).astype(q_ref.dtype)


def q_up(qkv_a, gains, layer_i, w_uq, cos, sin, name):
    m = qkv_a.shape[0]
    tm = _tile_rows(m)
    scale = float((QK_NOPE + QK_ROPE) ** -0.5)
    return pl.pallas_call(
        functools.partial(_q_up_kernel, scale),
        grid=(m // tm, MLA_HEADS),
        in_specs=[pl.BlockSpec((tm, Q_LORA), lambda i, h: (i, 0)),
                  _gain_spec(layer_i, Q_LORA, 2),
                  pl.BlockSpec((Q_LORA, HEAD_PAD), lambda i, h: (0, h)),
                  pl.BlockSpec((tm, V7X_LANES), lambda i, h: (i, 0)),
                  pl.BlockSpec((tm, V7X_LANES), lambda i, h: (i, 0))],
        out_specs=pl.BlockSpec((1, tm, HEAD_PAD), lambda i, h: (h, i, 0)),
        out_shape=jax.ShapeDtypeStruct((MLA_HEADS, m, HEAD_PAD), BF16),
        scratch_shapes=[pltpu.VMEM((tm, Q_LORA), BF16)],
        compiler_params=_params(("parallel", "arbitrary"), 32),
        name=name,
    )(qkv_a, gains, w_uq, cos, sin)


def _kv_up_kernel(c_ref, r_ref, g_ref, w_ref, cos_ref, sin_ref, k_ref, vt_ref, n_s, r_s):
    @pl.when(pl.program_id(1) == 0)
    def _():
        n_s[...] = _rms(c_ref[...], g_ref[0]).astype(BF16)
        r_s[...] = _rope_lanes(r_ref[...], cos_ref, sin_ref).astype(BF16)

    kv = _dot(n_s[...], w_ref[...])
    k_ref[0, :, :QK_NOPE] = kv[:, :QK_NOPE].astype(k_ref.dtype)
    k_ref[0, :, QK_NOPE:] = r_s[...]
    vt_ref[0] = kv[:, QK_NOPE:].T.astype(vt_ref.dtype)


def kv_up(qkv_a, gains, layer_i, w_ukv, cos, sin, name):
    m = qkv_a.shape[0]
    tm = _tile_rows(m)
    rope_blk = (Q_LORA + KV_LORA) // V7X_LANES
    return pl.pallas_call(
        _kv_up_kernel,
        grid=(m // tm, MLA_HEADS),
        in_specs=[pl.BlockSpec((tm, KV_LORA), lambda i, h: (i, 1)),
                  pl.BlockSpec((tm, V7X_LANES), lambda i, h: (i, rope_blk)),
                  _gain_spec(layer_i, KV_LORA, 2),
                  pl.BlockSpec((KV_LORA, QK_NOPE + V_DIM), lambda i, h: (0, h)),
                  pl.BlockSpec((tm, V7X_LANES), lambda i, h: (i, 0)),
                  pl.BlockSpec((tm, V7X_LANES), lambda i, h: (i, 0))],
        out_specs=[pl.BlockSpec((1, tm, HEAD_PAD), lambda i, h: (h, i, 0)),
                   pl.BlockSpec((1, V_DIM, tm), lambda i, h: (h, 0, i))],
        out_shape=[jax.ShapeDtypeStruct((MLA_HEADS, m, HEAD_PAD), BF16),
                   jax.ShapeDtypeStruct((MLA_HEADS, V_DIM, m), BF16)],
        scratch_shapes=[pltpu.VMEM((tm, KV_LORA), BF16), pltpu.VMEM((tm, V7X_LANES), BF16)],
        compiler_params=_params(("parallel", "arbitrary"), 32),
        name=name,
    )(qkv_a, qkv_a, gains, w_ukv, cos, sin)


def _flash_kernel(tk, n_chunks, has_ctx, *refs):
    if has_ctx:
        q_ref, k_ref, vt_ref, kc_ref, vtc_ref, o_ref, m_s, l_s, acc_s = refs
    else:
        q_ref, k_ref, vt_ref, o_ref, m_s, l_s, acc_s = refs
    q = q_ref[0]
    m_s[...] = jnp.full(m_s.shape, -jnp.inf, F32)
    l_s[...] = jnp.zeros(l_s.shape, F32)
    acc_s[...] = jnp.zeros(acc_s.shape, F32)

    def step(k, vt):
        s = lax.dot_general(k, q, (((1,), (1,)), ((), ())), preferred_element_type=F32)
        m_old = m_s[...]
        m_new = jnp.maximum(m_old, jnp.max(s, axis=0, keepdims=True))
        alpha = jnp.exp(m_old - m_new)
        p = jnp.exp(s - m_new)
        l_s[...] = alpha * l_s[...] + jnp.sum(p, axis=0, keepdims=True)
        acc_s[...] = alpha * acc_s[...] + _dot(vt, p.astype(BF16))
        m_s[...] = m_new

    def body(c, carry):
        off = pl.multiple_of(c * tk, tk)
        step(k_ref[0, pl.ds(off, tk), :], vt_ref[0, :, pl.ds(off, tk)])
        return carry

    lax.fori_loop(0, n_chunks, body, 0)
    if has_ctx:
        step(kc_ref[0], vtc_ref[0])
    o_ref[...] = (acc_s[...] / l_s[...]).T.astype(o_ref.dtype)


def flash_attention(q, k, vt, k_ctx, vt_ctx, name):
    h, m, _ = q.shape
    n_k = k.shape[1]
    tq = min(m, 512)
    tk = min(n_k, 1024)
    has_ctx = k_ctx is not None
    in_specs = [pl.BlockSpec((1, tq, HEAD_PAD), lambda hh, i: (hh, i, 0)),
                pl.BlockSpec((1, n_k, HEAD_PAD), lambda hh, i: (hh, 0, 0)),
                pl.BlockSpec((1, V_DIM, n_k), lambda hh, i: (hh, 0, 0))]
    args = [q, k, vt]
    if has_ctx:
        n_c = k_ctx.shape[1]
        in_specs += [pl.BlockSpec((1, n_c, HEAD_PAD), lambda hh, i: (hh, 0, 0)),
                     pl.BlockSpec((1, V_DIM, n_c), lambda hh, i: (hh, 0, 0))]
        args += [k_ctx, vt_ctx]
    return pl.pallas_call(
        functools.partial(_flash_kernel, tk, n_k // tk, has_ctx),
        grid=(h, m // tq),
        in_specs=in_specs,
        out_specs=pl.BlockSpec((tq, V_DIM), lambda hh, i: (i, hh)),
        out_shape=jax.ShapeDtypeStruct((m, h * V_DIM), BF16),
        scratch_shapes=[pltpu.VMEM((1, tq), F32), pltpu.VMEM((1, tq), F32), pltpu.VMEM((V_DIM, tq), F32)],
        compiler_params=_params(("parallel", "parallel"), 48),
        name=name,
    )(*args)


def _swap_pairs(w):
    q = QK_ROPE // 4
    r1, r2, c1, c2 = (w[..., j * q:(j + 1) * q] for j in range(4))
    return jnp.concatenate([-r2, r1, -c2, c1], axis=-1)


def _rope_tables(n):
    q = QK_ROPE // 4
    t = jnp.arange(n)
    inv = jnp.power(jnp.float32(ROPE_THETA), -jnp.arange(q, dtype=F32) / q)
    ar = (t // GRID_W).astype(F32)[:, None] * inv
    ac = (t % GRID_W).astype(F32)[:, None] * inv
    zeros = jnp.zeros((n, V7X_LANES - QK_ROPE), F32)
    cos_r, sin_r, cos_c, sin_c = jnp.cos(ar), jnp.sin(ar), jnp.cos(ac), jnp.sin(ac)
    cos = jnp.concatenate([cos_r, cos_r, cos_c, cos_c, zeros], axis=1)
    sin = jnp.concatenate([sin_r, sin_r, sin_c, sin_c, zeros], axis=1)
    return cos, sin


def _identity_tables(n):
    cos = jnp.concatenate([jnp.ones((n, QK_ROPE), F32), jnp.zeros((n, V7X_LANES - QK_ROPE), F32)], axis=1)
    return cos, jnp.zeros((n, V7X_LANES), F32)


def _mla_weights(w_dq, w_uq, w_dkv):
    rope = w_dkv[:, KV_LORA:]
    w_down = jnp.concatenate([w_dq, w_dkv[:, :KV_LORA], rope, _swap_pairs(rope)], axis=1).astype(BF16)
    wq = w_uq.reshape(Q_LORA, MLA_HEADS, QK_NOPE + QK_ROPE)
    wq_rope = wq[..., QK_NOPE:]
    w_up_q = jnp.concatenate([wq, _swap_pairs(wq_rope)], axis=-1).reshape(Q_LORA, MLA_HEADS * HEAD_PAD)
    return w_down, w_up_q.astype(BF16)


def kernel(x, c, ctx, c_ctx, ada_w, ada_b, norm1_g, norm2_g, even_w_in, pool_w, pool_scale, conv_w,
           even_w_out, mla_w_dq, mla_q_norm_g, mla_w_uq, mla_w_dkv, mla_kv_norm_g, mla_w_ukv, mla_w_o,
           ffn_w_gate, ffn_w_up, ffn_w_down, final_norm_g):
    n_lat, n_ctx = x.shape[1], ctx.shape[1]
    x_lat, x_ctx = x[0], ctx[0]

    cond8 = jnp.concatenate([c, c_ctx[None, :], jnp.zeros((V7X_SUBLANES - 2, D_MODEL), F32)], axis=0)
    mods = ada_table(cond8, ada_w, ada_b)
    g1 = norm1_g.reshape(DEPTH, 1, D_MODEL)
    g2 = norm2_g.reshape(DEPTH, 1, D_MODEL)
    gq = mla_q_norm_g.reshape(-1, 1, Q_LORA)
    gkv = mla_kv_norm_g.reshape(-1, 1, KV_LORA)
    final_g = final_norm_g.reshape(1, D_MODEL)
    rope_lat = _rope_tables(n_lat)
    rope_ctx = _identity_tables(n_ctx)

    for layer in range(DEPTH):
        last = layer == DEPTH - 1
        odd = layer % 2 == 1
        i = layer // 2
        streams = [("lat", LAT_ROW, x_lat)]
        if odd or not last:
            streams.append(("ctx", CTX_ROW, x_ctx))

        if odd:
            w_down, w_up_q = _mla_weights(mla_w_dq[i], mla_w_uq[i], mla_w_dkv[i])
            w_ukv = mla_w_ukv[i].astype(BF16)
            w_o = mla_w_o[i].astype(BF16)
            proj = {}
            for tag, row, xs in streams:
                cos, sin = rope_lat if tag == "lat" else rope_ctx
                a = normmod_matmul(xs, g1, mods, layer, row, w_down, 384, f"mla_down_{tag}")
                k, vt = kv_up(a, gkv, i, w_ukv, cos, sin, f"kv_up_{tag}")
                need_q = tag == "lat" or not last
                q = q_up(a, gq, i, w_up_q, cos, sin, f"q_up_{tag}") if need_q else None
                proj[tag] = (q, k, vt)
            q, k, vt = proj["lat"]
            qc, kc, vtc = proj["ctx"]
            att = flash_attention(q, k, vt, kc, vtc, "attn_lat")
            x_lat = matmul_residual(att, w_o, x_lat, mods, layer, LAT_ROW, "attn_out_lat")
            if not last:
                att_c = flash_attention(qc, kc, vtc, None, None, "attn_ctx")
                x_ctx = matmul_residual(att_c, w_o, x_ctx, mods, layer, CTX_ROW, "attn_out_ctx")
        else:
            w_in = even_w_in[i].astype(BF16)
            w_out = even_w_out[i].astype(BF16)
            pw = pool_w[i].astype(BF16)
            ps = pool_scale[i].reshape(1, POOL_WIDTH)
            outs = {}
            for tag, row, xs in streams:
                z = normmod_matmul(xs, g1, mods, layer, row, w_in, 512, f"even_in_{tag}")
                mid = even_mid(z, pw, ps, conv_w[i], f"even_mid_{tag}")
                outs[tag] = matmul_residual(mid, w_out, xs, mods, layer, row, f"even_out_{tag}")
            x_lat = outs["lat"]
            if "ctx" in outs:
                x_ctx = outs["ctx"]

        wg = ffn_w_gate[layer].astype(BF16)
        wu = ffn_w_up[layer].astype(BF16)
        wd = ffn_w_down[layer].astype(BF16)
        x_lat = ffn_residual(x_lat, g2, mods, layer, LAT_ROW, wg, wu, wd, final_g, last, "ffn_lat")
        if not last:
            x_ctx = ffn_residual(x_ctx, g2, mods, layer, CTX_ROW, wg, wu, wd, final_g, False, "ffn_ctx")

    return x_lat[None]
```

```python
import functools

import jax
import jax.numpy as jnp
import numpy as np
from jax import lax
from jax.experimental import pallas as pl
from jax.experimental.pallas import tpu as pltpu

D_MODEL = 2048
DEPTH = 4
GRID_W = 64
EPS = 1e-6
POOL_WINDOWS = (2, 4, 8, 16)
POOL_WIDTH = 1024
POOL_GROUP = 256
CONV_WIDTH = 1024
EVEN_IN = POOL_WIDTH + 3 * CONV_WIDTH
MLA_HEADS = 16
QK_NOPE = 128
QK_ROPE = 64
V_DIM = 128
Q_LORA = 512
KV_LORA = 512
ROPE_THETA = 10000.0
D_FF = 5632

V7X_LANES = 128
V7X_SUBLANES = 8
V7X_MXU_DIM = 256
V7X_VMEM_BYTES = 64 * 1024 * 1024

HEAD_PAD = V7X_MXU_DIM
BF16_SUBLANES = 2 * V7X_SUBLANES
VT_ROWS = V_DIM + BF16_SUBLANES
ATTN_PIECE = V7X_MXU_DIM
ATTN_AHEAD = 3
HALO = V7X_SUBLANES
LAT_ROW, CTX_ROW = 0, 1
BF16 = jnp.bfloat16
F32 = jnp.float32


def _params(semantics, vmem_mb, flags=None):
    return pltpu.CompilerParams(dimension_semantics=semantics,
                                vmem_limit_bytes=vmem_mb * 1024 * 1024, flags=flags)


def _tile_rows(m):
    return min(m, 1024)


def _dot(a, b):
    return jnp.dot(a, b, preferred_element_type=F32)


def _rms(x, g):
    return x * lax.rsqrt(jnp.mean(x * x, axis=-1, keepdims=True) + EPS) * g


def _mod_spec(layer, chunk, ngrid):
    if ngrid == 1:
        return pl.BlockSpec((1, V7X_SUBLANES, D_MODEL), lambda i: (layer, 0, chunk))
    return pl.BlockSpec((1, V7X_SUBLANES, D_MODEL), lambda i, j: (layer, 0, chunk))


def _gain_spec(layer, width, ngrid):
    if ngrid == 1:
        return pl.BlockSpec((1, 1, width), lambda i: (layer, 0, 0))
    return pl.BlockSpec((1, 1, width), lambda i, j: (layer, 0, 0))


def _ada_kernel(cond_ref, w_ref, b_ref, o_ref):
    c = cond_ref[...]
    s = (c * jax.nn.sigmoid(c)).astype(BF16)
    o_ref[0] = _dot(s, w_ref[0].astype(BF16)) + b_ref[0]


def ada_table(cond8, ada_w, ada_b):
    tn = 1024
    n_out = 6 * D_MODEL
    return pl.pallas_call(
        _ada_kernel,
        grid=(DEPTH, n_out // tn),
        in_specs=[pl.BlockSpec((V7X_SUBLANES, D_MODEL), lambda l, j: (0, 0)),
                  pl.BlockSpec((1, D_MODEL, tn), lambda l, j: (l, 0, j)),
                  pl.BlockSpec((1, 1, tn), lambda l, j: (l, 0, j))],
        out_specs=pl.BlockSpec((1, V7X_SUBLANES, tn), lambda l, j: (l, 0, j)),
        out_shape=jax.ShapeDtypeStruct((DEPTH, V7X_SUBLANES, n_out), F32),
        compiler_params=_params(("parallel", "parallel"), 40),
        name="ada_table",
    )(cond8, ada_w, ada_b.reshape(DEPTH, 1, n_out))


def _normmod_matmul_kernel(row, x_ref, g_ref, sh_ref, sc_ref, w_ref, o_ref, h_s):
    @pl.when(pl.program_id(1) == 0)
    def _():
        y = _rms(x_ref[...], g_ref[0])
        h = y * (1.0 + sc_ref[0, row:row + 1, :]) + sh_ref[0, row:row + 1, :]
        h_s[...] = h.astype(BF16)

    o_ref[...] = _dot(h_s[...], w_ref[...]).astype(o_ref.dtype)


def normmod_matmul(x, gains, mods, layer, row, w, tn, name):
    m = x.shape[0]
    tm = _tile_rows(m)
    n_out = w.shape[1]
    return pl.pallas_call(
        functools.partial(_normmod_matmul_kernel, row),
        grid=(m // tm, n_out // tn),
        in_specs=[pl.BlockSpec((tm, D_MODEL), lambda i, j: (i, 0)),
                  _gain_spec(layer, D_MODEL, 2),
                  _mod_spec(layer, 0, 2),
                  _mod_spec(layer, 1, 2),
                  pl.BlockSpec((D_MODEL, tn), lambda i, j: (0, j))],
        out_specs=pl.BlockSpec((tm, tn), lambda i, j: (i, j)),
        out_shape=jax.ShapeDtypeStruct((m, n_out), F32),
        scratch_shapes=[pltpu.VMEM((tm, D_MODEL), BF16)],
        compiler_params=_params(("parallel", "arbitrary"), 48),
        name=name,
    )(x, gains, mods, mods, w)


def _matmul_residual_kernel(row, a_ref, w_ref, x_ref, gate_ref, o_ref):
    y = _dot(a_ref[...], w_ref[...])
    o_ref[...] = x_ref[...] + gate_ref[0, row:row + 1, :] * y


def matmul_residual(a, w, x, mods, layer, row, name):
    m, k = a.shape
    tm = _tile_rows(m)
    tn = 512
    return pl.pallas_call(
        functools.partial(_matmul_residual_kernel, row),
        grid=(m // tm, D_MODEL // tn),
        in_specs=[pl.BlockSpec((tm, k), lambda i, j: (i, 0)),
                  pl.BlockSpec((k, tn), lambda i, j: (0, j)),
                  pl.BlockSpec((tm, tn), lambda i, j: (i, j)),
                  pl.BlockSpec((1, V7X_SUBLANES, tn), lambda i, j: (layer, 0, 2 * (D_MODEL // tn) + j))],
        out_specs=pl.BlockSpec((tm, tn), lambda i, j: (i, j)),
        out_shape=jax.ShapeDtypeStruct((m, D_MODEL), F32),
        compiler_params=_params(("parallel", "parallel"), 40),
        name=name,
    )(a, w, x, mods)


def _ffn_up_kernel(row, x_ref, g_ref, sh_ref, sc_ref, wg_ref, wu_ref, a_ref, h_s):
    @pl.when(pl.program_id(1) == 0)
    def _():
        y = _rms(x_ref[...], g_ref[0])
        h = y * (1.0 + sc_ref[0, row:row + 1, :]) + sh_ref[0, row:row + 1, :]
        h_s[...] = h.astype(BF16)

    h = h_s[...]
    g = _dot(h, wg_ref[...])
    u = _dot(h, wu_ref[...])
    a_ref[...] = (g * jax.nn.sigmoid(g) * u).astype(a_ref.dtype)


def _ffn_down_kernel(row, final_norm, a_ref, wd_ref, x_ref, gate_ref, fg_ref, o_ref):
    out = x_ref[...] + gate_ref[0, row:row + 1, :] * _dot(a_ref[...], wd_ref[...])
    if final_norm:
        out = _rms(out, fg_ref[...])
    o_ref[...] = out


def ffn_residual(x, gains, mods, layer, row, wg, wu, wd, final_g, final_norm, name):
    m = x.shape[0]
    tm = _tile_rows(m)
    tf = 512
    act = pl.pallas_call(
        functools.partial(_ffn_up_kernel, row),
        grid=(m // tm, D_FF // tf),
        in_specs=[pl.BlockSpec((tm, D_MODEL), lambda i, f: (i, 0)),
                  _gain_spec(layer, D_MODEL, 2),
                  _mod_spec(layer, 3, 2),
                  _mod_spec(layer, 4, 2),
                  pl.BlockSpec((D_MODEL, tf), lambda i, f: (0, f)),
                  pl.BlockSpec((D_MODEL, tf), lambda i, f: (0, f))],
        out_specs=pl.BlockSpec((tm, tf), lambda i, f: (i, f)),
        out_shape=jax.ShapeDtypeStruct((m, D_FF), BF16),
        scratch_shapes=[pltpu.VMEM((tm, D_MODEL), BF16)],
        compiler_params=_params(("parallel", "arbitrary"), 48),
        name=name + "_up",
    )(x, gains, mods, mods, wg, wu)
    tr = 256
    return pl.pallas_call(
        functools.partial(_ffn_down_kernel, row, final_norm),
        grid=(m // tr,),
        in_specs=[pl.BlockSpec((tr, D_FF), lambda i: (i, 0)),
                  pl.BlockSpec((D_FF, D_MODEL), lambda i: (0, 0), pipeline_mode=pl.Buffered(1)),
                  pl.BlockSpec((tr, D_MODEL), lambda i: (i, 0)),
                  _mod_spec(layer, 5, 1),
                  pl.BlockSpec((1, D_MODEL), lambda i: (0, 0))],
        out_specs=pl.BlockSpec((tr, D_MODEL), lambda i: (i, 0)),
        out_shape=jax.ShapeDtypeStruct((m, D_MODEL), F32),
        compiler_params=_params(("parallel",), 48),
        name=name + "_down",
    )(act, wd, x, mods, final_g)


def _shift_rows(x, k):
    n = x.shape[0]
    return pltpu.roll(x, (n - k) % n, axis=0)


def _even_mid_kernel(seq_len, zp_ref, z_ref, zn_ref, pw_ref, ps_ref, cw_ref, o_ref):
    i = pl.program_id(0)
    tm = z_ref.shape[0]
    first = i == 0
    last = i == pl.num_programs(0) - 1
    row = i * tm + lax.broadcasted_iota(jnp.int32, (tm, 1), 0)

    def with_halo(lo, hi):
        before = jnp.where(first, 0.0, zp_ref[:, lo:hi])
        after = jnp.where(last, 0.0, zn_ref[:, lo:hi])
        return jnp.concatenate([before, z_ref[:, lo:hi], after], axis=0)

    for g, w in enumerate(POOL_WINDOWS):
        lo_c, hi_c = g * POOL_GROUP, (g + 1) * POOL_GROUP
        u = with_halo(lo_c, hi_c)
        b = u
        s = 1
        while s < w:
            b = b + _shift_rows(b, s)
            s *= 2
        win = _shift_rows(b, -(w // 2))[HALO:HALO + tm]
        cnt = (jnp.minimum(row + (w - w // 2), seq_len) - jnp.maximum(row - w // 2, 0)).astype(F32)
        p = win / cnt - z_ref[:, lo_c:hi_c]
        y = _dot(p.astype(BF16), pw_ref[g]) * ps_ref[:, lo_c:hi_c]
        o_ref[:, lo_c:hi_c] = y.astype(o_ref.dtype)

    c0 = POOL_WIDTH
    gate_b = z_ref[:, c0:c0 + CONV_WIDTH]
    u = with_halo(c0 + CONV_WIDTH, c0 + 2 * CONV_WIDTH) * with_halo(c0 + 2 * CONV_WIDTH, c0 + 3 * CONV_WIDTH)
    conv = (_shift_rows(u, -1) * cw_ref[0:1, :] + u * cw_ref[1:2, :] + _shift_rows(u, 1) * cw_ref[2:3, :])
    y_b = gate_b * conv[HALO:HALO + tm]
    o_ref[:, POOL_WIDTH:] = y_b.astype(o_ref.dtype)


def even_mid(z, pool_w, pool_scale, conv_w, name):
    m = z.shape[0]
    tm = min(m, 512)
    nb = tm // HALO
    last_blk = m // HALO - 1
    return pl.pallas_call(
        functools.partial(_even_mid_kernel, m),
        grid=(m // tm,),
        in_specs=[pl.BlockSpec((HALO, EVEN_IN), lambda i: (jnp.maximum(i * nb - 1, 0), 0)),
                  pl.BlockSpec((tm, EVEN_IN), lambda i: (i, 0)),
                  pl.BlockSpec((HALO, EVEN_IN), lambda i: (jnp.minimum((i + 1) * nb, last_blk), 0)),
                  pl.BlockSpec((len(POOL_WINDOWS), POOL_GROUP, POOL_GROUP), lambda i: (0, 0, 0)),
                  pl.BlockSpec((1, POOL_WIDTH), lambda i: (0, 0)),
                  pl.BlockSpec((3, CONV_WIDTH), lambda i: (0, 0))],
        out_specs=pl.BlockSpec((tm, POOL_WIDTH + CONV_WIDTH), lambda i: (i, 0)),
        out_shape=jax.ShapeDtypeStruct((m, POOL_WIDTH + CONV_WIDTH), BF16),
        compiler_params=_params(("parallel",), 48),
        name=name,
    )(z, z, z, pool_w, pool_scale, conv_w)


def _rope_lanes(t, cos_ref, sin_ref):
    return t * cos_ref[...] + pltpu.roll(t, QK_ROPE, axis=1) * sin_ref[...]


def _q_up_kernel(scale, a_ref, g_ref, w_ref, cos_ref, sin_ref, qt_ref, n_s):
    @pl.when(pl.program_id(1) == 0)
    def _():
        n_s[...] = _rms(a_ref[...], g_ref[0]).astype(BF16)

    q = _dot(n_s[...], w_ref[...])
    rope = _rope_lanes(q[:, QK_NOPE:], cos_ref, sin_ref)
    qt_ref[0, :QK_NOPE, :] = (q[:, :QK_NOPE] * scale).T.astype(qt_ref.dtype)
    qt_ref[0, QK_NOPE:, :] = (rope * scale).T.astype(qt_ref.dtype)


def q_up(qkv_a, gains, layer_i, w_uq, cos, sin, name):
    m = qkv_a.shape[0]
    tm = _tile_rows(m)
    scale = float((QK_NOPE + QK_ROPE) ** -0.5 * np.log2(np.e))
    return pl.pallas_call(
        functools.partial(_q_up_kernel, scale),
        grid=(m // tm, MLA_HEADS),
        in_specs=[pl.BlockSpec((tm, Q_LORA), lambda i, h: (i, 0)),
                  _gain_spec(layer_i, Q_LORA, 2),
                  pl.BlockSpec((Q_LORA, HEAD_PAD), lambda i, h: (0, h)),
                  pl.BlockSpec((tm, V7X_LANES), lambda i, h: (i, 0)),
                  pl.BlockSpec((tm, V7X_LANES), lambda i, h: (i, 0))],
        out_specs=pl.BlockSpec((1, HEAD_PAD, tm), lambda i, h: (h, 0, i)),
        out_shape=jax.ShapeDtypeStruct((MLA_HEADS, HEAD_PAD, m), BF16),
        scratch_shapes=[pltpu.VMEM((tm, Q_LORA), BF16)],
        compiler_params=_params(("parallel", "arbitrary"), 32),
        name=name,
    )(qkv_a, gains, w_uq, cos, sin)


def _kv_up_kernel(c_ref, r_ref, g_ref, w_ref, cos_ref, sin_ref, k_ref, vt_ref, n_s, r_s):
    @pl.when(pl.program_id(1) == 0)
    def _():
        n_s[...] = _rms(c_ref[...], g_ref[0]).astype(BF16)
        r_s[...] = _rope_lanes(r_ref[...], cos_ref, sin_ref).astype(BF16)

    kv = _dot(n_s[...], w_ref[...])
    k_ref[0, :, :QK_NOPE] = kv[:, :QK_NOPE].astype(k_ref.dtype)
    k_ref[0, :, QK_NOPE:] = r_s[...]
    vt_ref[0, :V_DIM, :] = kv[:, QK_NOPE:].T.astype(vt_ref.dtype)
    vt_ref[0, V_DIM:, :] = jnp.ones((VT_ROWS - V_DIM, vt_ref.shape[2]), vt_ref.dtype)


def kv_up(qkv_a, gains, layer_i, w_ukv, cos, sin, name):
    m = qkv_a.shape[0]
    tm = _tile_rows(m)
    rope_blk = (Q_LORA + KV_LORA) // V7X_LANES
    return pl.pallas_call(
        _kv_up_kernel,
        grid=(m // tm, MLA_HEADS),
        in_specs=[pl.BlockSpec((tm, KV_LORA), lambda i, h: (i, 1)),
                  pl.BlockSpec((tm, V7X_LANES), lambda i, h: (i, rope_blk)),
                  _gain_spec(layer_i, KV_LORA, 2),
                  pl.BlockSpec((KV_LORA, QK_NOPE + V_DIM), lambda i, h: (0, h)),
                  pl.BlockSpec((tm, V7X_LANES), lambda i, h: (i, 0)),
                  pl.BlockSpec((tm, V7X_LANES), lambda i, h: (i, 0))],
        out_specs=[pl.BlockSpec((1, tm, HEAD_PAD), lambda i, h: (h, i, 0)),
                   pl.BlockSpec((1, VT_ROWS, tm), lambda i, h: (h, 0, i))],
        out_shape=[jax.ShapeDtypeStruct((MLA_HEADS, m, HEAD_PAD), BF16),
                   jax.ShapeDtypeStruct((MLA_HEADS, VT_ROWS, m), BF16)],
        scratch_shapes=[pltpu.VMEM((tm, KV_LORA), BF16), pltpu.VMEM((tm, V7X_LANES), BF16)],
        compiler_params=_params(("parallel", "arbitrary"), 32),
        name=name,
    )(qkv_a, qkv_a, gains, w_ukv, cos, sin)


def _flash_kernel(n_lat, has_ctx, *refs):
    if has_ctx:
        qt_ref, k_ref, vt_ref, kc_ref, vtc_ref, o_ref = refs
    else:
        qt_ref, k_ref, vt_ref, o_ref = refs
    qt = qt_ref[0]
    tq = qt.shape[1]
    n_all = n_lat + int(has_ctx)

    def keys(i):
        return kc_ref[0] if i >= n_lat else k_ref[0, i * ATTN_PIECE:(i + 1) * ATTN_PIECE, :]

    def values_t(i):
        return vtc_ref[0] if i >= n_lat else vt_ref[0, :, i * ATTN_PIECE:(i + 1) * ATTN_PIECE]

    m = jnp.full((1, tq), -jnp.inf, F32)
    acc = jnp.zeros((VT_ROWS, tq), F32)
    pending = [_dot(keys(i), qt) for i in range(min(ATTN_AHEAD, n_all))]
    for t in range(n_all):
        s = pending.pop(0)
        if t + ATTN_AHEAD < n_all:
            pending.append(_dot(keys(t + ATTN_AHEAD), qt))
        m_new = jnp.maximum(m, jnp.max(s, axis=0, keepdims=True))
        p = jnp.exp2(s - m_new).astype(BF16)
        acc = jnp.exp2(m - m_new) * acc + _dot(values_t(t), p)
        m = m_new
    o_ref[...] = (acc[:V_DIM] / acc[V_DIM:V_DIM + 1]).T.astype(o_ref.dtype)


def flash_attention(qt, k, vt, k_ctx, vt_ctx, name):
    h, _, m = qt.shape
    n_k = k.shape[1]
    tq = min(m, 512)
    has_ctx = k_ctx is not None
    in_specs = [pl.BlockSpec((1, HEAD_PAD, tq), lambda hh, i: (hh, 0, i)),
                pl.BlockSpec((1, n_k, HEAD_PAD), lambda hh, i: (hh, 0, 0)),
                pl.BlockSpec((1, VT_ROWS, n_k), lambda hh, i: (hh, 0, 0))]
    args = [qt, k, vt]
    if has_ctx:
        n_c = k_ctx.shape[1]
        in_specs += [pl.BlockSpec((1, n_c, HEAD_PAD), lambda hh, i: (hh, 0, 0)),
                     pl.BlockSpec((1, VT_ROWS, n_c), lambda hh, i: (hh, 0, 0))]
        args += [k_ctx, vt_ctx]
    return pl.pallas_call(
        functools.partial(_flash_kernel, n_k // ATTN_PIECE, has_ctx),
        grid=(h, m // tq),
        in_specs=in_specs,
        out_specs=pl.BlockSpec((tq, V_DIM), lambda hh, i: (i, hh)),
        out_shape=jax.ShapeDtypeStruct((m, h * V_DIM), BF16),
        compiler_params=_params(("parallel", "parallel"), 48),
        name=name,
    )(*args)


def _swap_pairs(w):
    q = QK_ROPE // 4
    r1, r2, c1, c2 = (w[..., j * q:(j + 1) * q] for j in range(4))
    return jnp.concatenate([-r2, r1, -c2, c1], axis=-1)


def _rope_tables(n):
    q = QK_ROPE // 4
    t = jnp.arange(n)
    inv = jnp.power(jnp.float32(ROPE_THETA), -jnp.arange(q, dtype=F32) / q)
    ar = (t // GRID_W).astype(F32)[:, None] * inv
    ac = (t % GRID_W).astype(F32)[:, None] * inv
    zeros = jnp.zeros((n, V7X_LANES - QK_ROPE), F32)
    cos_r, sin_r, cos_c, sin_c = jnp.cos(ar), jnp.sin(ar), jnp.cos(ac), jnp.sin(ac)
    cos = jnp.concatenate([cos_r, cos_r, cos_c, cos_c, zeros], axis=1)
    sin = jnp.concatenate([sin_r, sin_r, sin_c, sin_c, zeros], axis=1)
    return cos, sin


def _identity_tables(n):
    cos = jnp.concatenate([jnp.ones((n, QK_ROPE), F32), jnp.zeros((n, V7X_LANES - QK_ROPE), F32)], axis=1)
    return cos, jnp.zeros((n, V7X_LANES), F32)


def _mla_weights(w_dq, w_uq, w_dkv):
    rope = w_dkv[:, KV_LORA:]
    w_down = jnp.concatenate([w_dq, w_dkv[:, :KV_LORA], rope, _swap_pairs(rope)], axis=1).astype(BF16)
    wq = w_uq.reshape(Q_LORA, MLA_HEADS, QK_NOPE + QK_ROPE)
    wq_rope = wq[..., QK_NOPE:]
    w_up_q = jnp.concatenate([wq, _swap_pairs(wq_rope)], axis=-1).reshape(Q_LORA, MLA_HEADS * HEAD_PAD)
    return w_down, w_up_q.astype(BF16)


def kernel(x, c, ctx, c_ctx, ada_w, ada_b, norm1_g, norm2_g, even_w_in, pool_w, pool_scale, conv_w,
           even_w_out, mla_w_dq, mla_q_norm_g, mla_w_uq, mla_w_dkv, mla_kv_norm_g, mla_w_ukv, mla_w_o,
           ffn_w_gate, ffn_w_up, ffn_w_down, final_norm_g):
    n_lat, n_ctx = x.shape[1], ctx.shape[1]
    x_lat, x_ctx = x[0], ctx[0]

    cond8 = jnp.concatenate([c, c_ctx[None, :], jnp.zeros((V7X_SUBLANES - 2, D_MODEL), F32)], axis=0)
    mods = ada_table(cond8, ada_w, ada_b)
    g1 = norm1_g.reshape(DEPTH, 1, D_MODEL)
    g2 = norm2_g.reshape(DEPTH, 1, D_MODEL)
    gq = mla_q_norm_g.reshape(-1, 1, Q_LORA)
    gkv = mla_kv_norm_g.reshape(-1, 1, KV_LORA)
    final_g = final_norm_g.reshape(1, D_MODEL)
    rope_lat = _rope_tables(n_lat)
    rope_ctx = _identity_tables(n_ctx)

    for layer in range(DEPTH):
        last = layer == DEPTH - 1
        odd = layer % 2 == 1
        i = layer // 2
        streams = [("lat", LAT_ROW, x_lat)]
        if odd or not last:
            streams.append(("ctx", CTX_ROW, x_ctx))

        if odd:
            w_down, w_up_q = _mla_weights(mla_w_dq[i], mla_w_uq[i], mla_w_dkv[i])
            w_ukv = mla_w_ukv[i].astype(BF16)
            w_o = mla_w_o[i].astype(BF16)
            proj = {}
            for tag, row, xs in streams:
                cos, sin = rope_lat if tag == "lat" else rope_ctx
                a = normmod_matmul(xs, g1, mods, layer, row, w_down, 384, f"mla_down_{tag}")
                k, vt = kv_up(a, gkv, i, w_ukv, cos, sin, f"kv_up_{tag}")
                need_q = tag == "lat" or not last
                q = q_up(a, gq, i, w_up_q, cos, sin, f"q_up_{tag}") if need_q else None
                proj[tag] = (q, k, vt)
            q, k, vt = proj["lat"]
            qc, kc, vtc = proj["ctx"]
            att = flash_attention(q, k, vt, kc, vtc, "attn_lat")
            x_lat = matmul_residual(att, w_o, x_lat, mods, layer, LAT_ROW, "attn_out_lat")
            if not last:
                att_c = flash_attention(qc, kc, vtc, None, None, "attn_ctx")
                x_ctx = matmul_residual(att_c, w_o, x_ctx, mods, layer, CTX_ROW, "attn_out_ctx")
        else:
            w_in = even_w_in[i].astype(BF16)
            w_out = even_w_out[i].astype(BF16)
            pw = pool_w[i].astype(BF16)
            ps = pool_scale[i].reshape(1, POOL_WIDTH)
            outs = {}
            for tag, row, xs in streams:
                z = normmod_matmul(xs, g1, mods, layer, row, w_in, 512, f"even_in_{tag}")
                mid = even_mid(z, pw, ps, conv_w[i], f"even_mid_{tag}")
                outs[tag] = matmul_residual(mid, w_out, xs, mods, layer, row, f"even_out_{tag}")
            x_lat = outs["lat"]
            if "ctx" in outs:
                x_ctx = outs["ctx"]

        wg = ffn_w_gate[layer].astype(BF16)
        wu = ffn_w_up[layer].astype(BF16)
        wd = ffn_w_down[layer].astype(BF16)
        x_lat = ffn_residual(x_lat, g2, mods, layer, LAT_ROW, wg, wu, wd, final_g, last, "ffn_lat")
        if not last:
            x_ctx = ffn_residual(x_ctx, g2, mods, layer, CTX_ROW, wg, wu, wd, final_g, False, "ffn_ctx")

    return x_lat[None]
```

```python
import functools

import jax
import jax.numpy as jnp
import numpy as np
from jax import lax
from jax.experimental import pallas as pl
from jax.experimental.pallas import tpu as pltpu

D_MODEL = 2048
DEPTH = 4
GRID_W = 64
EPS = 1e-6
POOL_WINDOWS = (2, 4, 8, 16)
POOL_WIDTH = 1024
POOL_GROUP = 256
CONV_WIDTH = 1024
EVEN_IN = POOL_WIDTH + 3 * CONV_WIDTH
MLA_HEADS = 16
QK_NOPE = 128
QK_ROPE = 64
V_DIM = 128
Q_LORA = 512
KV_LORA = 512
ROPE_THETA = 10000.0
D_FF = 5632

V7X_LANES = 128
V7X_SUBLANES = 8
V7X_MXU_DIM = 256

HEAD_PAD = V7X_MXU_DIM
HALO = V7X_SUBLANES
LAT_ROW, CTX_ROW = 0, 1
BF16_SUBLANES = 2 * V7X_SUBLANES
VT_ROWS = V_DIM + BF16_SUBLANES
MLA_DOWN = Q_LORA + KV_LORA + 2 * QK_ROPE
ATTN_TQ = 1024
ATTN_PIECE = V7X_MXU_DIM
ATTN_AHEAD = 3
HEAD_ROWS = 256
BF16 = jnp.bfloat16
F32 = jnp.float32


def _params(semantics, vmem_mb):
    return pltpu.CompilerParams(dimension_semantics=semantics,
                                vmem_limit_bytes=vmem_mb * 1024 * 1024)


def _tile_rows(m):
    return min(m, 1024)


def _dot(a, b):
    return jnp.dot(a, b, preferred_element_type=F32)


def _rms(x, g):
    return x * lax.rsqrt(jnp.mean(x * x, axis=-1, keepdims=True) + EPS) * g


def _normmod(x_ref, g_ref, sh_ref, sc_ref, row):
    x = x_ref[...]
    inv = lax.rsqrt(jnp.mean(x * x, axis=-1, keepdims=True) + EPS)
    gain = g_ref[0] * (1.0 + sc_ref[0, row:row + 1, :])
    return (x * inv * gain + sh_ref[0, row:row + 1, :]).astype(BF16)


def _mod_spec(layer, chunk, ngrid):
    if ngrid == 1:
        return pl.BlockSpec((1, V7X_SUBLANES, D_MODEL), lambda i: (layer, 0, chunk))
    return pl.BlockSpec((1, V7X_SUBLANES, D_MODEL), lambda i, j: (layer, 0, chunk))


def _gain_spec(layer, width, ngrid):
    if ngrid == 1:
        return pl.BlockSpec((1, 1, width), lambda i: (layer, 0, 0))
    return pl.BlockSpec((1, 1, width), lambda i, j: (layer, 0, 0))


def _ada_kernel(cond_ref, w_ref, b_ref, o_ref):
    c = cond_ref[...]
    s = (c * jax.nn.sigmoid(c)).astype(BF16)
    o_ref[0] = _dot(s, w_ref[0].astype(BF16)) + b_ref[0]


def ada_table(cond8, ada_w, ada_b):
    tn = 1024
    n_out = 6 * D_MODEL
    return pl.pallas_call(
        _ada_kernel,
        grid=(DEPTH, n_out // tn),
        in_specs=[pl.BlockSpec((V7X_SUBLANES, D_MODEL), lambda l, j: (0, 0)),
                  pl.BlockSpec((1, D_MODEL, tn), lambda l, j: (l, 0, j)),
                  pl.BlockSpec((1, 1, tn), lambda l, j: (l, 0, j))],
        out_specs=pl.BlockSpec((1, V7X_SUBLANES, tn), lambda l, j: (l, 0, j)),
        out_shape=jax.ShapeDtypeStruct((DEPTH, V7X_SUBLANES, n_out), F32),
        compiler_params=_params(("parallel", "parallel"), 40),
        name="ada_table",
    )(cond8, ada_w, ada_b.reshape(DEPTH, 1, n_out))


def _normmod_matmul_kernel(row, x_ref, g_ref, sh_ref, sc_ref, w_ref, o_ref, h_s):
    @pl.when(pl.program_id(1) == 0)
    def _():
        h_s[...] = _normmod(x_ref, g_ref, sh_ref, sc_ref, row)

    o_ref[...] = _dot(h_s[...], w_ref[0]).astype(o_ref.dtype)


def normmod_matmul(x, gains, mods, layer, row, w, li, tn, name):
    m = x.shape[0]
    tm = _tile_rows(m)
    n_out = w.shape[2]
    return pl.pallas_call(
        functools.partial(_normmod_matmul_kernel, row),
        grid=(m // tm, n_out // tn),
        in_specs=[pl.BlockSpec((tm, D_MODEL), lambda i, j: (i, 0)),
                  _gain_spec(layer, D_MODEL, 2),
                  _mod_spec(layer, 0, 2),
                  _mod_spec(layer, 1, 2),
                  pl.BlockSpec((1, D_MODEL, tn), lambda i, j: (li, 0, j))],
        out_specs=pl.BlockSpec((tm, tn), lambda i, j: (i, j)),
        out_shape=jax.ShapeDtypeStruct((m, n_out), F32),
        scratch_shapes=[pltpu.VMEM((tm, D_MODEL), BF16)],
        compiler_params=_params(("parallel", "arbitrary"), 48),
        name=name,
    )(x, gains, mods, mods, w)


def _matmul_residual_kernel(row, a_ref, w_ref, x_ref, gate_ref, o_ref):
    y = _dot(a_ref[...], w_ref[0])
    o_ref[...] = x_ref[...] + gate_ref[0, row:row + 1, :] * y


def matmul_residual(a, w, li, x, mods, layer, row, name):
    m, k = a.shape
    tm = _tile_rows(m)
    tn = 512
    return pl.pallas_call(
        functools.partial(_matmul_residual_kernel, row),
        grid=(m // tm, D_MODEL // tn),
        in_specs=[pl.BlockSpec((tm, k), lambda i, j: (i, 0)),
                  pl.BlockSpec((1, k, tn), lambda i, j: (li, 0, j)),
                  pl.BlockSpec((tm, tn), lambda i, j: (i, j)),
                  pl.BlockSpec((1, V7X_SUBLANES, tn), lambda i, j: (layer, 0, 2 * (D_MODEL // tn) + j))],
        out_specs=pl.BlockSpec((tm, tn), lambda i, j: (i, j)),
        out_shape=jax.ShapeDtypeStruct((m, D_MODEL), F32),
        compiler_params=_params(("parallel", "parallel"), 40),
        name=name,
    )(a, w, x, mods)


def _ffn_up_kernel(row, x_ref, g_ref, sh_ref, sc_ref, wg_ref, wu_ref, a_ref, h_s):
    @pl.when(pl.program_id(1) == 0)
    def _():
        h_s[...] = _normmod(x_ref, g_ref, sh_ref, sc_ref, row)

    h = h_s[...]
    g = _dot(h, wg_ref[0])
    u = _dot(h, wu_ref[0])
    a_ref[...] = (g * jax.nn.sigmoid(g) * u).astype(a_ref.dtype)


def _ffn_down_kernel(row, final_norm, a_ref, wd_ref, x_ref, gate_ref, fg_ref, o_ref):
    out = x_ref[...] + gate_ref[0, row:row + 1, :] * _dot(a_ref[...], wd_ref[0])
    if final_norm:
        out = _rms(out, fg_ref[...])
    o_ref[...] = out


def ffn_residual(x, gains, mods, layer, row, wg, wu, wd, final_g, final_norm, name):
    m = x.shape[0]
    tm = _tile_rows(m)
    tf = 512
    act = pl.pallas_call(
        functools.partial(_ffn_up_kernel, row),
        grid=(m // tm, D_FF // tf),
        in_specs=[pl.BlockSpec((tm, D_MODEL), lambda i, f: (i, 0)),
                  _gain_spec(layer, D_MODEL, 2),
                  _mod_spec(layer, 3, 2),
                  _mod_spec(layer, 4, 2),
                  pl.BlockSpec((1, D_MODEL, tf), lambda i, f: (layer, 0, f)),
                  pl.BlockSpec((1, D_MODEL, tf), lambda i, f: (layer, 0, f))],
        out_specs=pl.BlockSpec((tm, tf), lambda i, f: (i, f)),
        out_shape=jax.ShapeDtypeStruct((m, D_FF), BF16),
        scratch_shapes=[pltpu.VMEM((tm, D_MODEL), BF16)],
        compiler_params=_params(("parallel", "arbitrary"), 48),
        name=name + "_up",
    )(x, gains, mods, mods, wg, wu)
    tr = HEAD_ROWS
    return pl.pallas_call(
        functools.partial(_ffn_down_kernel, row, final_norm),
        grid=(m // tr,),
        in_specs=[pl.BlockSpec((tr, D_FF), lambda i: (i, 0)),
                  pl.BlockSpec((1, D_FF, D_MODEL), lambda i: (layer, 0, 0), pipeline_mode=pl.Buffered(1)),
                  pl.BlockSpec((tr, D_MODEL), lambda i: (i, 0)),
                  _mod_spec(layer, 5, 1),
                  pl.BlockSpec((1, D_MODEL), lambda i: (0, 0))],
        out_specs=pl.BlockSpec((tr, D_MODEL), lambda i: (i, 0)),
        out_shape=jax.ShapeDtypeStruct((m, D_MODEL), F32),
        compiler_params=_params(("parallel",), 48),
        name=name + "_down",
    )(act, wd, x, mods, final_g)


def _shift_rows(x, k):
    n = x.shape[0]
    return pltpu.roll(x, (n - k) % n, axis=0)


def _even_mid_kernel(seq_len, zp_ref, z_ref, zn_ref, pw_ref, ps_ref, cw_ref, o_ref):
    i = pl.program_id(0)
    tm = z_ref.shape[0]
    first = i == 0
    last = i == pl.num_programs(0) - 1
    row = i * tm + lax.broadcasted_iota(jnp.int32, (tm, 1), 0)

    def with_halo(lo, hi):
        before = jnp.where(first, 0.0, zp_ref[:, lo:hi])
        after = jnp.where(last, 0.0, zn_ref[:, lo:hi])
        return jnp.concatenate([before, z_ref[:, lo:hi], after], axis=0)

    for g, w in enumerate(POOL_WINDOWS):
        lo_c, hi_c = g * POOL_GROUP, (g + 1) * POOL_GROUP
        u = with_halo(lo_c, hi_c)
        b = u
        s = 1
        while s < w:
            b = b + _shift_rows(b, s)
            s *= 2
        win = _shift_rows(b, -(w // 2))[HALO:HALO + tm]
        cnt = (jnp.minimum(row + (w - w // 2), seq_len) - jnp.maximum(row - w // 2, 0)).astype(F32)
        p = win / cnt - z_ref[:, lo_c:hi_c]
        y = _dot(p.astype(BF16), pw_ref[0, g]) * ps_ref[0, :, lo_c:hi_c]
        o_ref[:, lo_c:hi_c] = y.astype(o_ref.dtype)

    c0 = POOL_WIDTH
    gate_b = z_ref[:, c0:c0 + CONV_WIDTH]
    u = with_halo(c0 + CONV_WIDTH, c0 + 2 * CONV_WIDTH) * with_halo(c0 + 2 * CONV_WIDTH, c0 + 3 * CONV_WIDTH)
    conv = (_shift_rows(u, -1) * cw_ref[0, 0:1, :] + u * cw_ref[0, 1:2, :] + _shift_rows(u, 1) * cw_ref[0, 2:3, :])
    y_b = gate_b * conv[HALO:HALO + tm]
    o_ref[:, POOL_WIDTH:] = y_b.astype(o_ref.dtype)


def even_mid(z, pool_w, pool_scale, conv_w, li, name):
    m = z.shape[0]
    tm = min(m, 512)
    nb = tm // HALO
    last_blk = m // HALO - 1
    n_win = len(POOL_WINDOWS)
    return pl.pallas_call(
        functools.partial(_even_mid_kernel, m),
        grid=(m // tm,),
        in_specs=[pl.BlockSpec((HALO, EVEN_IN), lambda i: (jnp.maximum(i * nb - 1, 0), 0)),
                  pl.BlockSpec((tm, EVEN_IN), lambda i: (i, 0)),
                  pl.BlockSpec((HALO, EVEN_IN), lambda i: (jnp.minimum((i + 1) * nb, last_blk), 0)),
                  pl.BlockSpec((1, n_win, POOL_GROUP, POOL_GROUP), lambda i: (li, 0, 0, 0)),
                  pl.BlockSpec((1, 1, POOL_WIDTH), lambda i: (li, 0, 0)),
                  pl.BlockSpec((1, 3, CONV_WIDTH), lambda i: (li, 0, 0))],
        out_specs=pl.BlockSpec((tm, POOL_WIDTH + CONV_WIDTH), lambda i: (i, 0)),
        out_shape=jax.ShapeDtypeStruct((m, POOL_WIDTH + CONV_WIDTH), BF16),
        compiler_params=_params(("parallel",), 48),
        name=name,
    )(z, z, z, pool_w, pool_scale, conv_w)


def _rope_lanes(t, cos_ref, sin_ref):
    return t * cos_ref[...] + pltpu.roll(t, QK_ROPE, axis=1) * sin_ref[...]


def _q_up_kernel(scale, a_ref, g_ref, w_ref, cos_ref, sin_ref, qt_ref):
    n = _rms(a_ref[...], g_ref[0]).astype(BF16)
    q = _dot(n, w_ref[0])
    for h in range(MLA_HEADS):
        base = h * HEAD_PAD
        rope = _rope_lanes(q[:, base + QK_NOPE:base + HEAD_PAD], cos_ref, sin_ref)
        qt_ref[h, :QK_NOPE, :] = (q[:, base:base + QK_NOPE] * scale).T.astype(qt_ref.dtype)
        qt_ref[h, QK_NOPE:, :] = (rope * scale).T.astype(qt_ref.dtype)


def q_up(qkv_a, gains, li, w_uq, cos, sin, name):
    m = qkv_a.shape[0]
    tr = HEAD_ROWS
    scale = float((QK_NOPE + QK_ROPE) ** -0.5 * np.log2(np.e))
    return pl.pallas_call(
        functools.partial(_q_up_kernel, scale),
        grid=(m // tr,),
        in_specs=[pl.BlockSpec((tr, Q_LORA), lambda i: (i, 0)),
                  _gain_spec(li, Q_LORA, 1),
                  pl.BlockSpec((1, Q_LORA, MLA_HEADS * HEAD_PAD), lambda i: (li, 0, 0)),
                  pl.BlockSpec((tr, V7X_LANES), lambda i: (i, 0)),
                  pl.BlockSpec((tr, V7X_LANES), lambda i: (i, 0))],
        out_specs=pl.BlockSpec((MLA_HEADS, HEAD_PAD, tr), lambda i: (0, 0, i)),
        out_shape=jax.ShapeDtypeStruct((MLA_HEADS, HEAD_PAD, m), BF16),
        compiler_params=_params(("parallel",), 40),
        name=name,
    )(qkv_a, gains, w_uq, cos, sin)


def _kv_up_kernel(c_ref, r_ref, g_ref, w_ref, cos_ref, sin_ref, k_ref, vt_ref):
    n = _rms(c_ref[...], g_ref[0]).astype(BF16)
    k_rope = _rope_lanes(r_ref[...], cos_ref, sin_ref).astype(k_ref.dtype)
    kv = _dot(n, w_ref[0])
    ones = jnp.ones((VT_ROWS - V_DIM, vt_ref.shape[2]), vt_ref.dtype)
    width = QK_NOPE + V_DIM
    for h in range(MLA_HEADS):
        base = h * width
        k_ref[h, :, :QK_NOPE] = kv[:, base:base + QK_NOPE].astype(k_ref.dtype)
        k_ref[h, :, QK_NOPE:] = k_rope
        vt_ref[h, :V_DIM, :] = kv[:, base + QK_NOPE:base + width].T.astype(vt_ref.dtype)
        vt_ref[h, V_DIM:, :] = ones


def kv_up(qkv_a, gains, li, w_ukv, cos, sin, name):
    m = qkv_a.shape[0]
    tr = HEAD_ROWS
    rope_blk = (Q_LORA + KV_LORA) // V7X_LANES
    return pl.pallas_call(
        _kv_up_kernel,
        grid=(m // tr,),
        in_specs=[pl.BlockSpec((tr, KV_LORA), lambda i: (i, 1)),
                  pl.BlockSpec((tr, V7X_LANES), lambda i: (i, rope_blk)),
                  _gain_spec(li, KV_LORA, 1),
                  pl.BlockSpec((1, KV_LORA, MLA_HEADS * (QK_NOPE + V_DIM)), lambda i: (li, 0, 0)),
                  pl.BlockSpec((tr, V7X_LANES), lambda i: (i, 0)),
                  pl.BlockSpec((tr, V7X_LANES), lambda i: (i, 0))],
        out_specs=[pl.BlockSpec((MLA_HEADS, tr, HEAD_PAD), lambda i: (0, i, 0)),
                   pl.BlockSpec((MLA_HEADS, VT_ROWS, tr), lambda i: (0, 0, i))],
        out_shape=[jax.ShapeDtypeStruct((MLA_HEADS, m, HEAD_PAD), BF16),
                   jax.ShapeDtypeStruct((MLA_HEADS, VT_ROWS, m), BF16)],
        compiler_params=_params(("parallel",), 40),
        name=name,
    )(qkv_a, qkv_a, gains, w_ukv, cos, sin)


def _flash_kernel(piece, n_lat, has_ctx, *refs):
    if has_ctx:
        qt_ref, k_ref, vt_ref, kc_ref, vtc_ref, o_ref = refs
    else:
        qt_ref, k_ref, vt_ref, o_ref = refs
    qt = qt_ref[0]
    tq = qt.shape[1]
    n_all = n_lat + int(has_ctx)

    def keys(i):
        return kc_ref[0] if i >= n_lat else k_ref[0, i * piece:(i + 1) * piece, :]

    def values_t(i):
        return vtc_ref[0] if i >= n_lat else vt_ref[0, :, i * piece:(i + 1) * piece]

    m = jnp.full((1, tq), -jnp.inf, F32)
    acc = jnp.zeros((VT_ROWS, tq), F32)
    pending = [_dot(keys(i), qt) for i in range(min(ATTN_AHEAD, n_all))]
    for t in range(n_all):
        s = pending.pop(0)
        if t + ATTN_AHEAD < n_all:
            pending.append(_dot(keys(t + ATTN_AHEAD), qt))
        m_new = jnp.maximum(m, jnp.max(s, axis=0, keepdims=True))
        p = jnp.exp2(s - m_new).astype(BF16)
        acc = jnp.exp2(m - m_new) * acc + _dot(values_t(t), p)
        m = m_new
    o_ref[...] = (acc[:V_DIM] / acc[V_DIM:V_DIM + 1]).T.astype(o_ref.dtype)


def flash_attention(qt, k, vt, k_ctx, vt_ctx, name):
    h, _, m = qt.shape
    n_k = k.shape[1]
    tq = min(m, ATTN_TQ)
    piece = min(n_k, ATTN_PIECE)
    has_ctx = k_ctx is not None
    in_specs = [pl.BlockSpec((1, HEAD_PAD, tq), lambda hh, i: (hh, 0, i)),
                pl.BlockSpec((1, n_k, HEAD_PAD), lambda hh, i: (hh, 0, 0)),
                pl.BlockSpec((1, VT_ROWS, n_k), lambda hh, i: (hh, 0, 0))]
    args = [qt, k, vt]
    if has_ctx:
        n_c = k_ctx.shape[1]
        in_specs += [pl.BlockSpec((1, n_c, HEAD_PAD), lambda hh, i: (hh, 0, 0)),
                     pl.BlockSpec((1, VT_ROWS, n_c), lambda hh, i: (hh, 0, 0))]
        args += [k_ctx, vt_ctx]
    return pl.pallas_call(
        functools.partial(_flash_kernel, piece, n_k // piece, has_ctx),
        grid=(h, m // tq),
        in_specs=in_specs,
        out_specs=pl.BlockSpec((tq, V_DIM), lambda hh, i: (i, hh)),
        out_shape=jax.ShapeDtypeStruct((m, h * V_DIM), BF16),
        compiler_params=_params(("parallel", "parallel"), 48),
        name=name,
    )(*args)


def _swap_pairs(w):
    q = QK_ROPE // 4
    r1, r2, c1, c2 = (w[..., j * q:(j + 1) * q] for j in range(4))
    return jnp.concatenate([-r2, r1, -c2, c1], axis=-1)


def _rope_tables(n):
    q = QK_ROPE // 4
    rows = n // GRID_W
    inv = jnp.power(jnp.float32(ROPE_THETA), -jnp.arange(q, dtype=F32) / q)
    ar = jnp.arange(rows).astype(F32)[:, None] * inv
    ac = jnp.arange(GRID_W).astype(F32)[:, None] * inv
    cos_r, sin_r = (jnp.repeat(f(ar), GRID_W, axis=0) for f in (jnp.cos, jnp.sin))
    cos_c, sin_c = (jnp.tile(f(ac), (rows, 1)) for f in (jnp.cos, jnp.sin))
    zeros = jnp.zeros((n, V7X_LANES - QK_ROPE), F32)
    cos = jnp.concatenate([cos_r, cos_r, cos_c, cos_c, zeros], axis=1)
    sin = jnp.concatenate([sin_r, sin_r, sin_c, sin_c, zeros], axis=1)
    return cos, sin


def _identity_tables(n):
    cos = jnp.concatenate([jnp.ones((n, QK_ROPE), F32), jnp.zeros((n, V7X_LANES - QK_ROPE), F32)], axis=1)
    return cos, jnp.zeros((n, V7X_LANES), F32)


def _mla_weights(w_dq, w_uq, w_dkv):
    n = w_dq.shape[0]
    rope = w_dkv[..., KV_LORA:]
    w_down = jnp.concatenate([w_dq, w_dkv[..., :KV_LORA], rope, _swap_pairs(rope)], axis=-1).astype(BF16)
    wq = w_uq.reshape(n, Q_LORA, MLA_HEADS, QK_NOPE + QK_ROPE)
    w_up_q = jnp.concatenate([wq, _swap_pairs(wq[..., QK_NOPE:])], axis=-1)
    return w_down, w_up_q.reshape(n, Q_LORA, MLA_HEADS * HEAD_PAD).astype(BF16)


def kernel(x, c, ctx, c_ctx, ada_w, ada_b, norm1_g, norm2_g, even_w_in, pool_w, pool_scale, conv_w,
           even_w_out, mla_w_dq, mla_q_norm_g, mla_w_uq, mla_w_dkv, mla_kv_norm_g, mla_w_ukv, mla_w_o,
           ffn_w_gate, ffn_w_up, ffn_w_down, final_norm_g):
    n_lat, n_ctx = x.shape[1], ctx.shape[1]
    x_lat, x_ctx = x[0], ctx[0]

    cond8 = jnp.concatenate([c, c_ctx[None, :], jnp.zeros((V7X_SUBLANES - 2, D_MODEL), F32)], axis=0)
    mods = ada_table(cond8, ada_w, ada_b)
    g1 = norm1_g.reshape(DEPTH, 1, D_MODEL)
    g2 = norm2_g.reshape(DEPTH, 1, D_MODEL)
    gq = mla_q_norm_g.reshape(-1, 1, Q_LORA)
    gkv = mla_kv_norm_g.reshape(-1, 1, KV_LORA)
    final_g = final_norm_g.reshape(1, D_MODEL)
    rope_lat = _rope_tables(n_lat)
    rope_ctx = _identity_tables(n_ctx)

    w_in, w_out, pw = even_w_in.astype(BF16), even_w_out.astype(BF16), pool_w.astype(BF16)
    ps = pool_scale.reshape(-1, 1, POOL_WIDTH)
    w_down, w_up_q = _mla_weights(mla_w_dq, mla_w_uq, mla_w_dkv)
    w_ukv, w_o = mla_w_ukv.astype(BF16), mla_w_o.astype(BF16)
    wg, wu, wd = ffn_w_gate.astype(BF16), ffn_w_up.astype(BF16), ffn_w_down.astype(BF16)

    for layer in range(DEPTH):
        last = layer == DEPTH - 1
        odd = layer % 2 == 1
        li = layer // 2
        streams = [("lat", LAT_ROW, x_lat)]
        if odd or not last:
            streams.append(("ctx", CTX_ROW, x_ctx))

        if odd:
            proj = {}
            for tag, row, xs in streams:
                cos, sin = rope_lat if tag == "lat" else rope_ctx
                a = normmod_matmul(xs, g1, mods, layer, row, w_down, li, MLA_DOWN, f"mla_down_{tag}")
                k, vt = kv_up(a, gkv, li, w_ukv, cos, sin, f"kv_up_{tag}")
                need_q = tag == "lat" or not last
                qt = q_up(a, gq, li, w_up_q, cos, sin, f"q_up_{tag}") if need_q else None
                proj[tag] = (qt, k, vt)
            qt, k, vt = proj["lat"]
            qtc, kc, vtc = proj["ctx"]
            att = flash_attention(qt, k, vt, kc, vtc, "attn_lat")
            x_lat = matmul_residual(att, w_o, li, x_lat, mods, layer, LAT_ROW, "attn_out_lat")
            if not last:
                att_c = flash_attention(qtc, kc, vtc, None, None, "attn_ctx")
                x_ctx = matmul_residual(att_c, w_o, li, x_ctx, mods, layer, CTX_ROW, "attn_out_ctx")
        else:
            outs = {}
            for tag, row, xs in streams:
                z = normmod_matmul(xs, g1, mods, layer, row, w_in, li, 512, f"even_in_{tag}")
                mid = even_mid(z, pw, ps, conv_w, li, f"even_mid_{tag}")
                outs[tag] = matmul_residual(mid, w_out, li, xs, mods, layer, row, f"even_out_{tag}")
            x_lat = outs["lat"]
            if "ctx" in outs:
                x_ctx = outs["ctx"]

        x_lat = ffn_residual(x_lat, g2, mods, layer, LAT_ROW, wg, wu, wd, final_g, last, "ffn_lat")
        if not last:
            x_ctx = ffn_residual(x_ctx, g2, mods, layer, CTX_ROW, wg, wu, wd, final_g, False, "ffn_ctx")

    return x_lat[None]
```

```python
import functools

import jax
import jax.numpy as jnp
import numpy as np
from jax import lax
from jax.experimental import pallas as pl
from jax.experimental.pallas import tpu as pltpu

D_MODEL = 2048
DEPTH = 4
GRID_W = 64
EPS = 1e-6
POOL_WINDOWS = (2, 4, 8, 16)
POOL_WIDTH = 1024
POOL_GROUP = 256
CONV_WIDTH = 1024
EVEN_IN = POOL_WIDTH + 3 * CONV_WIDTH
MLA_HEADS = 16
QK_NOPE = 128
QK_ROPE = 64
V_DIM = 128
Q_LORA = 512
KV_LORA = 512
ROPE_THETA = 10000.0
D_FF = 5632

V7X_LANES = 128
V7X_SUBLANES = 8
V7X_MXU_DIM = 256

HEAD_PAD = V7X_MXU_DIM
HALO = V7X_SUBLANES
LAT_ROW, CTX_ROW = 0, 1
BF16_SUBLANES = 2 * V7X_SUBLANES
VT_ROWS = V_DIM + BF16_SUBLANES
MLA_DOWN = Q_LORA + KV_LORA + 2 * QK_ROPE
ATTN_TQ = 1024
ATTN_PIECE = V7X_MXU_DIM
ATTN_AHEAD = 3
HEAD_ROWS = 256
BF16 = jnp.bfloat16
F32 = jnp.float32


def _params(semantics, vmem_mb):
    return pltpu.CompilerParams(dimension_semantics=semantics,
                                vmem_limit_bytes=vmem_mb * 1024 * 1024)


def _tile_rows(m):
    return min(m, 1024)


def _dot(a, b):
    return jnp.dot(a, b, preferred_element_type=F32)


def _rms(x, g):
    return x * lax.rsqrt(jnp.mean(x * x, axis=-1, keepdims=True) + EPS) * g


def _normmod(x_ref, g_ref, sh_ref, sc_ref, row):
    x = x_ref[...]
    inv = lax.rsqrt(jnp.mean(x * x, axis=-1, keepdims=True) + EPS)
    gain = g_ref[0] * (1.0 + sc_ref[0, row:row + 1, :])
    return (x * inv * gain + sh_ref[0, row:row + 1, :]).astype(BF16)


def _mod_spec(layer, chunk, ngrid):
    if ngrid == 1:
        return pl.BlockSpec((1, V7X_SUBLANES, D_MODEL), lambda i: (layer, 0, chunk))
    return pl.BlockSpec((1, V7X_SUBLANES, D_MODEL), lambda i, j: (layer, 0, chunk))


def _gain_spec(layer, width, ngrid):
    if ngrid == 1:
        return pl.BlockSpec((1, 1, width), lambda i: (layer, 0, 0))
    return pl.BlockSpec((1, 1, width), lambda i, j: (layer, 0, 0))


def _ada_kernel(cond_ref, w_ref, b_ref, o_ref):
    c = cond_ref[...]
    s = (c * jax.nn.sigmoid(c)).astype(BF16)
    o_ref[0] = _dot(s, w_ref[0].astype(BF16)) + b_ref[0]


def ada_table(cond8, ada_w, ada_b):
    tn = 1024
    n_out = 6 * D_MODEL
    return pl.pallas_call(
        _ada_kernel,
        grid=(DEPTH, n_out // tn),
        in_specs=[pl.BlockSpec((V7X_SUBLANES, D_MODEL), lambda l, j: (0, 0)),
                  pl.BlockSpec((1, D_MODEL, tn), lambda l, j: (l, 0, j)),
                  pl.BlockSpec((1, 1, tn), lambda l, j: (l, 0, j))],
        out_specs=pl.BlockSpec((1, V7X_SUBLANES, tn), lambda l, j: (l, 0, j)),
        out_shape=jax.ShapeDtypeStruct((DEPTH, V7X_SUBLANES, n_out), F32),
        compiler_params=_params(("parallel", "parallel"), 40),
        name="ada_table",
    )(cond8, ada_w, ada_b.reshape(DEPTH, 1, n_out))


def _normmod_matmul_kernel(row, x_ref, g_ref, sh_ref, sc_ref, w_ref, o_ref, h_s):
    @pl.when(pl.program_id(1) == 0)
    def _():
        h_s[...] = _normmod(x_ref, g_ref, sh_ref, sc_ref, row)

    o_ref[...] = _dot(h_s[...], w_ref[0]).astype(o_ref.dtype)


def normmod_matmul(x, gains, mods, layer, row, w, li, tn, name):
    m = x.shape[0]
    tm = _tile_rows(m)
    n_out = w.shape[2]
    return pl.pallas_call(
        functools.partial(_normmod_matmul_kernel, row),
        grid=(m // tm, n_out // tn),
        in_specs=[pl.BlockSpec((tm, D_MODEL), lambda i, j: (i, 0)),
                  _gain_spec(layer, D_MODEL, 2),
                  _mod_spec(layer, 0, 2),
                  _mod_spec(layer, 1, 2),
                  pl.BlockSpec((1, D_MODEL, tn), lambda i, j: (li, 0, j))],
        out_specs=pl.BlockSpec((tm, tn), lambda i, j: (i, j)),
        out_shape=jax.ShapeDtypeStruct((m, n_out), F32),
        scratch_shapes=[pltpu.VMEM((tm, D_MODEL), BF16)],
        compiler_params=_params(("parallel", "arbitrary"), 48),
        name=name,
    )(x, gains, mods, mods, w)


def _normmod_kernel(row, x_ref, g_ref, sh_ref, sc_ref, h_ref):
    h_ref[...] = _normmod(x_ref, g_ref, sh_ref, sc_ref, row)


def normmod(x, gains, mods, layer, row, chunk, name):
    m = x.shape[0]
    tm = min(m, 512)
    return pl.pallas_call(
        functools.partial(_normmod_kernel, row),
        grid=(m // tm,),
        in_specs=[pl.BlockSpec((tm, D_MODEL), lambda i: (i, 0)),
                  _gain_spec(layer, D_MODEL, 1),
                  _mod_spec(layer, chunk, 1),
                  _mod_spec(layer, chunk + 1, 1)],
        out_specs=pl.BlockSpec((tm, D_MODEL), lambda i: (i, 0)),
        out_shape=jax.ShapeDtypeStruct((m, D_MODEL), BF16),
        compiler_params=_params(("parallel",), 32),
        name=name,
    )(x, gains, mods, mods)


def _round_weights(pairs):
    @pl.when(pl.program_id(1) == 0)
    def _():
        for w_ref, w_s in pairs:
            w_s[...] = w_ref[0].astype(BF16)


def _wcast_matmul_kernel(h_ref, w_ref, o_ref, w_s):
    _round_weights([(w_ref, w_s)])
    o_ref[...] = _dot(h_ref[...], w_s[...])


def wcast_matmul(h, w, li, name):
    m, k = h.shape
    n_out = w.shape[2]
    tm = _tile_rows(m)
    return pl.pallas_call(
        _wcast_matmul_kernel,
        grid=(1, m // tm),
        in_specs=[pl.BlockSpec((tm, k), lambda j, i: (i, 0)),
                  pl.BlockSpec((1, k, n_out), lambda j, i: (li, 0, 0), pipeline_mode=pl.Buffered(1))],
        out_specs=pl.BlockSpec((tm, n_out), lambda j, i: (i, 0)),
        out_shape=jax.ShapeDtypeStruct((m, n_out), F32),
        scratch_shapes=[pltpu.VMEM((k, n_out), BF16)],
        compiler_params=_params(("parallel", "arbitrary"), 48),
        name=name,
    )(h, w)


def _matmul_residual_kernel(row, a_ref, w_ref, x_ref, gate_ref, o_ref, w_s):
    _round_weights([(w_ref, w_s)])
    o_ref[...] = x_ref[...] + gate_ref[0, row:row + 1, :] * _dot(a_ref[...], w_s[...])


def matmul_residual(a, w, li, x, mods, layer, row, name):
    m, k = a.shape
    tm = min(m, 512)
    return pl.pallas_call(
        functools.partial(_matmul_residual_kernel, row),
        grid=(1, m // tm),
        in_specs=[pl.BlockSpec((tm, k), lambda j, i: (i, 0)),
                  pl.BlockSpec((1, k, D_MODEL), lambda j, i: (li, 0, 0), pipeline_mode=pl.Buffered(1)),
                  pl.BlockSpec((tm, D_MODEL), lambda j, i: (i, 0)),
                  _mod_spec(layer, 2, 2)],
        out_specs=pl.BlockSpec((tm, D_MODEL), lambda j, i: (i, 0)),
        out_shape=jax.ShapeDtypeStruct((m, D_MODEL), F32),
        scratch_shapes=[pltpu.VMEM((k, D_MODEL), BF16)],
        compiler_params=_params(("parallel", "arbitrary"), 52),
        name=name,
    )(a, w, x, mods)


def _ffn_up_kernel(h_ref, wg_ref, wu_ref, a_ref, wg_s, wu_s):
    _round_weights([(wg_ref, wg_s), (wu_ref, wu_s)])
    h = h_ref[...]
    g = _dot(h, wg_s[...])
    u = _dot(h, wu_s[...])
    a_ref[...] = (g * jax.nn.sigmoid(g) * u).astype(a_ref.dtype)


def _ffn_down_kernel(row, final_norm, a_ref, wd_ref, x_ref, gate_ref, fg_ref, o_ref):
    out = x_ref[...] + gate_ref[0, row:row + 1, :] * _dot(a_ref[...], wd_ref[0])
    if final_norm:
        out = _rms(out, fg_ref[...])
    o_ref[...] = out


def ffn_residual(x, gains, mods, layer, row, wg, wu, wd, final_g, final_norm, name):
    m = x.shape[0]
    tm = _tile_rows(m)
    tf = 512
    h = normmod(x, gains, mods, layer, row, 3, name + "_norm")
    act = pl.pallas_call(
        _ffn_up_kernel,
        grid=(D_FF // tf, m // tm),
        in_specs=[pl.BlockSpec((tm, D_MODEL), lambda f, i: (i, 0)),
                  pl.BlockSpec((1, D_MODEL, tf), lambda f, i: (layer, 0, f)),
                  pl.BlockSpec((1, D_MODEL, tf), lambda f, i: (layer, 0, f))],
        out_specs=pl.BlockSpec((tm, tf), lambda f, i: (i, f)),
        out_shape=jax.ShapeDtypeStruct((m, D_FF), BF16),
        scratch_shapes=[pltpu.VMEM((D_MODEL, tf), BF16), pltpu.VMEM((D_MODEL, tf), BF16)],
        compiler_params=_params(("parallel", "arbitrary"), 48),
        name=name + "_up",
    )(h, wg, wu)
    tr = HEAD_ROWS
    return pl.pallas_call(
        functools.partial(_ffn_down_kernel, row, final_norm),
        grid=(m // tr,),
        in_specs=[pl.BlockSpec((tr, D_FF), lambda i: (i, 0)),
                  pl.BlockSpec((1, D_FF, D_MODEL), lambda i: (layer, 0, 0), pipeline_mode=pl.Buffered(1)),
                  pl.BlockSpec((tr, D_MODEL), lambda i: (i, 0)),
                  _mod_spec(layer, 5, 1),
                  pl.BlockSpec((1, D_MODEL), lambda i: (0, 0))],
        out_specs=pl.BlockSpec((tr, D_MODEL), lambda i: (i, 0)),
        out_shape=jax.ShapeDtypeStruct((m, D_MODEL), F32),
        compiler_params=_params(("parallel",), 48),
        name=name + "_down",
    )(act, wd, x, mods, final_g)


def _shift_rows(x, k):
    n = x.shape[0]
    return pltpu.roll(x, (n - k) % n, axis=0)


def _even_mid_kernel(seq_len, zp_ref, z_ref, zn_ref, pw_ref, ps_ref, cw_ref, o_ref):
    i = pl.program_id(0)
    tm = z_ref.shape[0]
    first = i == 0
    last = i == pl.num_programs(0) - 1
    row = i * tm + lax.broadcasted_iota(jnp.int32, (tm, 1), 0)

    def with_halo(lo, hi):
        before = jnp.where(first, 0.0, zp_ref[:, lo:hi])
        after = jnp.where(last, 0.0, zn_ref[:, lo:hi])
        return jnp.concatenate([before, z_ref[:, lo:hi], after], axis=0)

    for g, w in enumerate(POOL_WINDOWS):
        lo_c, hi_c = g * POOL_GROUP, (g + 1) * POOL_GROUP
        u = with_halo(lo_c, hi_c)
        b = u
        s = 1
        while s < w:
            b = b + _shift_rows(b, s)
            s *= 2
        win = _shift_rows(b, -(w // 2))[HALO:HALO + tm]
        cnt = (jnp.minimum(row + (w - w // 2), seq_len) - jnp.maximum(row - w // 2, 0)).astype(F32)
        p = win / cnt - z_ref[:, lo_c:hi_c]
        y = _dot(p.astype(BF16), pw_ref[0, g]) * ps_ref[0, :, lo_c:hi_c]
        o_ref[:, lo_c:hi_c] = y.astype(o_ref.dtype)

    c0 = POOL_WIDTH
    gate_b = z_ref[:, c0:c0 + CONV_WIDTH]
    u = with_halo(c0 + CONV_WIDTH, c0 + 2 * CONV_WIDTH) * with_halo(c0 + 2 * CONV_WIDTH, c0 + 3 * CONV_WIDTH)
    conv = (_shift_rows(u, -1) * cw_ref[0, 0:1, :] + u * cw_ref[0, 1:2, :] + _shift_rows(u, 1) * cw_ref[0, 2:3, :])
    y_b = gate_b * conv[HALO:HALO + tm]
    o_ref[:, POOL_WIDTH:] = y_b.astype(o_ref.dtype)


def even_mid(z, pool_w, pool_scale, conv_w, li, name):
    m = z.shape[0]
    tm = min(m, 512)
    nb = tm // HALO
    last_blk = m // HALO - 1
    n_win = len(POOL_WINDOWS)
    return pl.pallas_call(
        functools.partial(_even_mid_kernel, m),
        grid=(m // tm,),
        in_specs=[pl.BlockSpec((HALO, EVEN_IN), lambda i: (jnp.maximum(i * nb - 1, 0), 0)),
                  pl.BlockSpec((tm, EVEN_IN), lambda i: (i, 0)),
                  pl.BlockSpec((HALO, EVEN_IN), lambda i: (jnp.minimum((i + 1) * nb, last_blk), 0)),
                  pl.BlockSpec((1, n_win, POOL_GROUP, POOL_GROUP), lambda i: (li, 0, 0, 0)),
                  pl.BlockSpec((1, 1, POOL_WIDTH), lambda i: (li, 0, 0)),
                  pl.BlockSpec((1, 3, CONV_WIDTH), lambda i: (li, 0, 0))],
        out_specs=pl.BlockSpec((tm, POOL_WIDTH + CONV_WIDTH), lambda i: (i, 0)),
        out_shape=jax.ShapeDtypeStruct((m, POOL_WIDTH + CONV_WIDTH), BF16),
        compiler_params=_params(("parallel",), 48),
        name=name,
    )(z, z, z, pool_w, pool_scale, conv_w)


def _rope_lanes(t, cos_ref, sin_ref):
    return t * cos_ref[...] + pltpu.roll(t, QK_ROPE, axis=1) * sin_ref[...]


def _q_up_kernel(scale, a_ref, g_ref, w_ref, cos_ref, sin_ref, qt_ref):
    n = _rms(a_ref[...], g_ref[0]).astype(BF16)
    q = _dot(n, w_ref[0])
    for h in range(MLA_HEADS):
        base = h * HEAD_PAD
        rope = _rope_lanes(q[:, base + QK_NOPE:base + HEAD_PAD], cos_ref, sin_ref)
        qt_ref[h, :QK_NOPE, :] = (q[:, base:base + QK_NOPE] * scale).T.astype(qt_ref.dtype)
        qt_ref[h, QK_NOPE:, :] = (rope * scale).T.astype(qt_ref.dtype)


def q_up(qkv_a, gains, li, w_uq, cos, sin, name):
    m = qkv_a.shape[0]
    tr = HEAD_ROWS
    scale = float((QK_NOPE + QK_ROPE) ** -0.5 * np.log2(np.e))
    return pl.pallas_call(
        functools.partial(_q_up_kernel, scale),
        grid=(m // tr,),
        in_specs=[pl.BlockSpec((tr, Q_LORA), lambda i: (i, 0)),
                  _gain_spec(li, Q_LORA, 1),
                  pl.BlockSpec((1, Q_LORA, MLA_HEADS * HEAD_PAD), lambda i: (li, 0, 0)),
                  pl.BlockSpec((tr, V7X_LANES), lambda i: (i, 0)),
                  pl.BlockSpec((tr, V7X_LANES), lambda i: (i, 0))],
        out_specs=pl.BlockSpec((MLA_HEADS, HEAD_PAD, tr), lambda i: (0, 0, i)),
        out_shape=jax.ShapeDtypeStruct((MLA_HEADS, HEAD_PAD, m), BF16),
        compiler_params=_params(("parallel",), 40),
        name=name,
    )(qkv_a, gains, w_uq, cos, sin)


def _kv_up_kernel(c_ref, r_ref, g_ref, w_ref, cos_ref, sin_ref, k_ref, vt_ref):
    n = _rms(c_ref[...], g_ref[0]).astype(BF16)
    k_rope = _rope_lanes(r_ref[...], cos_ref, sin_ref).astype(k_ref.dtype)
    kv = _dot(n, w_ref[0])
    ones = jnp.ones((VT_ROWS - V_DIM, vt_ref.shape[2]), vt_ref.dtype)
    width = QK_NOPE + V_DIM
    for h in range(MLA_HEADS):
        base = h * width
        k_ref[h, :, :QK_NOPE] = kv[:, base:base + QK_NOPE].astype(k_ref.dtype)
        k_ref[h, :, QK_NOPE:] = k_rope
        vt_ref[h, :V_DIM, :] = kv[:, base + QK_NOPE:base + width].T.astype(vt_ref.dtype)
        vt_ref[h, V_DIM:, :] = ones


def kv_up(qkv_a, gains, li, w_ukv, cos, sin, name):
    m = qkv_a.shape[0]
    tr = HEAD_ROWS
    rope_blk = (Q_LORA + KV_LORA) // V7X_LANES
    return pl.pallas_call(
        _kv_up_kernel,
        grid=(m // tr,),
        in_specs=[pl.BlockSpec((tr, KV_LORA), lambda i: (i, 1)),
                  pl.BlockSpec((tr, V7X_LANES), lambda i: (i, rope_blk)),
                  _gain_spec(li, KV_LORA, 1),
                  pl.BlockSpec((1, KV_LORA, MLA_HEADS * (QK_NOPE + V_DIM)), lambda i: (li, 0, 0)),
                  pl.BlockSpec((tr, V7X_LANES), lambda i: (i, 0)),
                  pl.BlockSpec((tr, V7X_LANES), lambda i: (i, 0))],
        out_specs=[pl.BlockSpec((MLA_HEADS, tr, HEAD_PAD), lambda i: (0, i, 0)),
                   pl.BlockSpec((MLA_HEADS, VT_ROWS, tr), lambda i: (0, 0, i))],
        out_shape=[jax.ShapeDtypeStruct((MLA_HEADS, m, HEAD_PAD), BF16),
                   jax.ShapeDtypeStruct((MLA_HEADS, VT_ROWS, m), BF16)],
        compiler_params=_params(("parallel",), 40),
        name=name,
    )(qkv_a, qkv_a, gains, w_ukv, cos, sin)


def _flash_kernel(piece, n_lat, has_ctx, *refs):
    if has_ctx:
        qt_ref, k_ref, vt_ref, kc_ref, vtc_ref, o_ref = refs
    else:
        qt_ref, k_ref, vt_ref, o_ref = refs
    qt = qt_ref[0]
    tq = qt.shape[1]
    n_all = n_lat + int(has_ctx)

    def keys(i):
        return kc_ref[0] if i >= n_lat else k_ref[0, i * piece:(i + 1) * piece, :]

    def values_t(i):
        return vtc_ref[0] if i >= n_lat else vt_ref[0, :, i * piece:(i + 1) * piece]

    m = jnp.full((1, tq), -jnp.inf, F32)
    acc = jnp.zeros((VT_ROWS, tq), F32)
    pending = [_dot(keys(i), qt) for i in range(min(ATTN_AHEAD, n_all))]
    for t in range(n_all):
        s = pending.pop(0)
        if t + ATTN_AHEAD < n_all:
            pending.append(_dot(keys(t + ATTN_AHEAD), qt))
        m_new = jnp.maximum(m, jnp.max(s, axis=0, keepdims=True))
        p = jnp.exp2(s - m_new).astype(BF16)
        acc = jnp.exp2(m - m_new) * acc + _dot(values_t(t), p)
        m = m_new
    o_ref[...] = (acc[:V_DIM] / acc[V_DIM:V_DIM + 1]).T.astype(o_ref.dtype)


def flash_attention(qt, k, vt, k_ctx, vt_ctx, name):
    h, _, m = qt.shape
    n_k = k.shape[1]
    tq = min(m, ATTN_TQ)
    piece = min(n_k, ATTN_PIECE)
    has_ctx = k_ctx is not None
    in_specs = [pl.BlockSpec((1, HEAD_PAD, tq), lambda hh, i: (hh, 0, i)),
                pl.BlockSpec((1, n_k, HEAD_PAD), lambda hh, i: (hh, 0, 0)),
                pl.BlockSpec((1, VT_ROWS, n_k), lambda hh, i: (hh, 0, 0))]
    args = [qt, k, vt]
    if has_ctx:
        n_c = k_ctx.shape[1]
        in_specs += [pl.BlockSpec((1, n_c, HEAD_PAD), lambda hh, i: (hh, 0, 0)),
                     pl.BlockSpec((1, VT_ROWS, n_c), lambda hh, i: (hh, 0, 0))]
        args += [k_ctx, vt_ctx]
    return pl.pallas_call(
        functools.partial(_flash_kernel, piece, n_k // piece, has_ctx),
        grid=(h, m // tq),
        in_specs=in_specs,
        out_specs=pl.BlockSpec((tq, V_DIM), lambda hh, i: (i, hh)),
        out_shape=jax.ShapeDtypeStruct((m, h * V_DIM), BF16),
        compiler_params=_params(("parallel", "parallel"), 48),
        name=name,
    )(*args)


def _swap_pairs(w):
    q = QK_ROPE // 4
    r1, r2, c1, c2 = (w[..., j * q:(j + 1) * q] for j in range(4))
    return jnp.concatenate([-r2, r1, -c2, c1], axis=-1)


def _rope_tables(n):
    q = QK_ROPE // 4
    rows = n // GRID_W
    inv = jnp.power(jnp.float32(ROPE_THETA), -jnp.arange(q, dtype=F32) / q)
    ar = jnp.arange(rows).astype(F32)[:, None] * inv
    ac = jnp.arange(GRID_W).astype(F32)[:, None] * inv
    cos_r, sin_r = (jnp.repeat(f(ar), GRID_W, axis=0) for f in (jnp.cos, jnp.sin))
    cos_c, sin_c = (jnp.tile(f(ac), (rows, 1)) for f in (jnp.cos, jnp.sin))
    zeros = jnp.zeros((n, V7X_LANES - QK_ROPE), F32)
    cos = jnp.concatenate([cos_r, cos_r, cos_c, cos_c, zeros], axis=1)
    sin = jnp.concatenate([sin_r, sin_r, sin_c, sin_c, zeros], axis=1)
    return cos, sin


def _identity_tables(n):
    cos = jnp.concatenate([jnp.ones((n, QK_ROPE), F32), jnp.zeros((n, V7X_LANES - QK_ROPE), F32)], axis=1)
    return cos, jnp.zeros((n, V7X_LANES), F32)


def _mla_weights(w_dq, w_uq, w_dkv):
    n = w_dq.shape[0]
    rope = w_dkv[..., KV_LORA:]
    w_down = jnp.concatenate([w_dq, w_dkv[..., :KV_LORA], rope, _swap_pairs(rope)], axis=-1)
    wq = w_uq.reshape(n, Q_LORA, MLA_HEADS, QK_NOPE + QK_ROPE)
    w_up_q = jnp.concatenate([wq, _swap_pairs(wq[..., QK_NOPE:])], axis=-1)
    return w_down, w_up_q.reshape(n, Q_LORA, MLA_HEADS * HEAD_PAD).astype(BF16)


def kernel(x, c, ctx, c_ctx, ada_w, ada_b, norm1_g, norm2_g, even_w_in, pool_w, pool_scale, conv_w,
           even_w_out, mla_w_dq, mla_q_norm_g, mla_w_uq, mla_w_dkv, mla_kv_norm_g, mla_w_ukv, mla_w_o,
           ffn_w_gate, ffn_w_up, ffn_w_down, final_norm_g):
    n_lat, n_ctx = x.shape[1], ctx.shape[1]
    x_lat, x_ctx = x[0], ctx[0]

    cond8 = jnp.concatenate([c, c_ctx[None, :], jnp.zeros((V7X_SUBLANES - 2, D_MODEL), F32)], axis=0)
    mods = ada_table(cond8, ada_w, ada_b)
    g1 = norm1_g.reshape(DEPTH, 1, D_MODEL)
    g2 = norm2_g.reshape(DEPTH, 1, D_MODEL)
    gq = mla_q_norm_g.reshape(-1, 1, Q_LORA)
    gkv = mla_kv_norm_g.reshape(-1, 1, KV_LORA)
    final_g = final_norm_g.reshape(1, D_MODEL)
    rope_lat = _rope_tables(n_lat)
    rope_ctx = _identity_tables(n_ctx)

    w_in, pw = even_w_in.astype(BF16), pool_w.astype(BF16)
    ps = pool_scale.reshape(-1, 1, POOL_WIDTH)
    w_down, w_up_q = _mla_weights(mla_w_dq, mla_w_uq, mla_w_dkv)
    w_ukv = mla_w_ukv.astype(BF16)
    wd = ffn_w_down.astype(BF16)

    for layer in range(DEPTH):
        last = layer == DEPTH - 1
        odd = layer % 2 == 1
        li = layer // 2
        streams = [("lat", LAT_ROW, x_lat)]
        if odd or not last:
            streams.append(("ctx", CTX_ROW, x_ctx))

        if odd:
            proj = {}
            for tag, row, xs in streams:
                cos, sin = rope_lat if tag == "lat" else rope_ctx
                h = normmod(xs, g1, mods, layer, row, 0, f"mla_norm_{tag}")
                a = wcast_matmul(h, w_down, li, f"mla_down_{tag}")
                k, vt = kv_up(a, gkv, li, w_ukv, cos, sin, f"kv_up_{tag}")
                need_q = tag == "lat" or not last
                qt = q_up(a, gq, li, w_up_q, cos, sin, f"q_up_{tag}") if need_q else None
                proj[tag] = (qt, k, vt)
            qt, k, vt = proj["lat"]
            qtc, kc, vtc = proj["ctx"]
            att = flash_attention(qt, k, vt, kc, vtc, "attn_lat")
            x_lat = matmul_residual(att, mla_w_o, li, x_lat, mods, layer, LAT_ROW, "attn_out_lat")
            if not last:
                att_c = flash_attention(qtc, kc, vtc, None, None, "attn_ctx")
                x_ctx = matmul_residual(att_c, mla_w_o, li, x_ctx, mods, layer, CTX_ROW, "attn_out_ctx")
        else:
            outs = {}
            for tag, row, xs in streams:
                z = normmod_matmul(xs, g1, mods, layer, row, w_in, li, 512, f"even_in_{tag}")
                mid = even_mid(z, pw, ps, conv_w, li, f"even_mid_{tag}")
                outs[tag] = matmul_residual(mid, even_w_out, li, xs, mods, layer, row, f"even_out_{tag}")
            x_lat = outs["lat"]
            if "ctx" in outs:
                x_ctx = outs["ctx"]

        x_lat = ffn_residual(x_lat, g2, mods, layer, LAT_ROW, ffn_w_gate, ffn_w_up, wd, final_g, last, "ffn_lat")
        if not last:
            x_ctx = ffn_residual(x_ctx, g2, mods, layer, CTX_ROW, ffn_w_gate, ffn_w_up, wd, final_g, False,
                                 "ffn_ctx")

    return x_lat[None]
```

```python
import functools

import jax
import jax.numpy as jnp
import numpy as np
from jax import lax
from jax.experimental import pallas as pl
from jax.experimental.pallas import tpu as pltpu

D_MODEL = 2048
DEPTH = 4
GRID_W = 64
EPS = 1e-6
POOL_WINDOWS = (2, 4, 8, 16)
POOL_WIDTH = 1024
POOL_GROUP = 256
CONV_WIDTH = 1024
EVEN_IN = POOL_WIDTH + 3 * CONV_WIDTH
MLA_HEADS = 16
QK_NOPE = 128
QK_ROPE = 64
V_DIM = 128
Q_LORA = 512
KV_LORA = 512
ROPE_THETA = 10000.0
D_FF = 5632

V7X_LANES = 128
V7X_SUBLANES = 8
V7X_MXU_DIM = 256

HEAD_PAD = V7X_MXU_DIM
HALO = V7X_SUBLANES
LAT_ROW, CTX_ROW = 0, 1
BF16_SUBLANES = 2 * V7X_SUBLANES
VT_ROWS = V_DIM + BF16_SUBLANES
ATTN_TQ = 1024
ATTN_PIECE = V7X_MXU_DIM
ATTN_AHEAD = 3
HEAD_ROWS = 256
BF16 = jnp.bfloat16
F32 = jnp.float32


def _params(semantics, vmem_mb):
    return pltpu.CompilerParams(dimension_semantics=semantics,
                                vmem_limit_bytes=vmem_mb * 1024 * 1024)


def _tile_rows(m):
    return min(m, 1024)


def _dot(a, b):
    return jnp.dot(a, b, preferred_element_type=F32)


def _rms(x, g):
    return x * lax.rsqrt(jnp.mean(x * x, axis=-1, keepdims=True) + EPS) * g


def _normmod(x, g_ref, sh_ref, sc_ref, row):
    inv = lax.rsqrt(jnp.mean(x * x, axis=-1, keepdims=True) + EPS)
    gain = g_ref[0] * (1.0 + sc_ref[0, row:row + 1, :])
    return (x * inv * gain + sh_ref[0, row:row + 1, :]).astype(BF16)


def _mod_spec(layer, chunk, ngrid):
    if ngrid == 1:
        return pl.BlockSpec((1, V7X_SUBLANES, D_MODEL), lambda i: (layer, 0, chunk))
    return pl.BlockSpec((1, V7X_SUBLANES, D_MODEL), lambda i, j: (layer, 0, chunk))


def _gain_spec(layer, width, ngrid):
    if ngrid == 1:
        return pl.BlockSpec((1, 1, width), lambda i: (layer, 0, 0))
    return pl.BlockSpec((1, 1, width), lambda i, j: (layer, 0, 0))


def _ada_kernel(cond_ref, w_ref, b_ref, o_ref):
    c = cond_ref[...]
    s = (c * jax.nn.sigmoid(c)).astype(BF16)
    o_ref[0] = _dot(s, w_ref[0].astype(BF16)) + b_ref[0]


def ada_table(cond8, ada_w, ada_b):
    tn = 1024
    n_out = 6 * D_MODEL
    return pl.pallas_call(
        _ada_kernel,
        grid=(DEPTH, n_out // tn),
        in_specs=[pl.BlockSpec((V7X_SUBLANES, D_MODEL), lambda l, j: (0, 0)),
                  pl.BlockSpec((1, D_MODEL, tn), lambda l, j: (l, 0, j)),
                  pl.BlockSpec((1, 1, tn), lambda l, j: (l, 0, j))],
        out_specs=pl.BlockSpec((1, V7X_SUBLANES, tn), lambda l, j: (l, 0, j)),
        out_shape=jax.ShapeDtypeStruct((DEPTH, V7X_SUBLANES, n_out), F32),
        compiler_params=_params(("parallel", "parallel"), 40),
        name="ada_table",
    )(cond8, ada_w, ada_b.reshape(DEPTH, 1, n_out))


def _matmul_cols_kernel(h_ref, w_ref, o_ref):
    o_ref[...] = _dot(h_ref[...], w_ref[0])


def matmul_cols(h, w, li, tn, name):
    m, k = h.shape
    tm = _tile_rows(m)
    n_out = w.shape[2]
    return pl.pallas_call(
        _matmul_cols_kernel,
        grid=(m // tm, n_out // tn),
        in_specs=[pl.BlockSpec((tm, k), lambda i, j: (i, 0)),
                  pl.BlockSpec((1, k, tn), lambda i, j: (li, 0, j))],
        out_specs=pl.BlockSpec((tm, tn), lambda i, j: (i, j)),
        out_shape=jax.ShapeDtypeStruct((m, n_out), F32),
        compiler_params=_params(("parallel", "parallel"), 40),
        name=name,
    )(h, w)


def _normmod_kernel(row, x_ref, g_ref, sh_ref, sc_ref, h_ref):
    h_ref[...] = _normmod(x_ref[...], g_ref, sh_ref, sc_ref, row)


def normmod(x, gains, mods, layer, row, chunk, name):
    m = x.shape[0]
    tm = min(m, 512)
    return pl.pallas_call(
        functools.partial(_normmod_kernel, row),
        grid=(m // tm,),
        in_specs=[pl.BlockSpec((tm, D_MODEL), lambda i: (i, 0)),
                  _gain_spec(layer, D_MODEL, 1),
                  _mod_spec(layer, chunk, 1),
                  _mod_spec(layer, chunk + 1, 1)],
        out_specs=pl.BlockSpec((tm, D_MODEL), lambda i: (i, 0)),
        out_shape=jax.ShapeDtypeStruct((m, D_MODEL), BF16),
        compiler_params=_params(("parallel",), 32),
        name=name,
    )(x, gains, mods, mods)


def _round_weights(pairs):
    @pl.when(pl.program_id(1) == 0)
    def _():
        for w_ref, w_s in pairs:
            w_s[...] = w_ref[0].astype(BF16)


def _wcast_matmul_kernel(h_ref, w_ref, o_ref, w_s):
    _round_weights([(w_ref, w_s)])
    o_ref[...] = _dot(h_ref[...], w_s[...])


def wcast_matmul(h, w, li, name):
    m, k = h.shape
    n_out = w.shape[2]
    tm = _tile_rows(m)
    return pl.pallas_call(
        _wcast_matmul_kernel,
        grid=(1, m // tm),
        in_specs=[pl.BlockSpec((tm, k), lambda j, i: (i, 0)),
                  pl.BlockSpec((1, k, n_out), lambda j, i: (li, 0, 0), pipeline_mode=pl.Buffered(1))],
        out_specs=pl.BlockSpec((tm, n_out), lambda j, i: (i, 0)),
        out_shape=jax.ShapeDtypeStruct((m, n_out), F32),
        scratch_shapes=[pltpu.VMEM((k, n_out), BF16)],
        compiler_params=_params(("parallel", "arbitrary"), 48),
        name=name,
    )(h, w)


def _matmul_residual_kernel(row, a_ref, w_ref, x_ref, gate_ref, g_ref, sh_ref, sc_ref, o_ref, h_ref, w_s):
    _round_weights([(w_ref, w_s)])
    out = x_ref[...] + gate_ref[0, row:row + 1, :] * _dot(a_ref[...], w_s[...])
    o_ref[...] = out
    h_ref[...] = _normmod(out, g_ref, sh_ref, sc_ref, row)


def matmul_residual(a, w, li, x, mods, layer, row, gains2, name):
    m, k = a.shape
    tm = min(m, 512)
    row_spec = pl.BlockSpec((tm, D_MODEL), lambda j, i: (i, 0))
    return pl.pallas_call(
        functools.partial(_matmul_residual_kernel, row),
        grid=(1, m // tm),
        in_specs=[pl.BlockSpec((tm, k), lambda j, i: (i, 0)),
                  pl.BlockSpec((1, k, D_MODEL), lambda j, i: (li, 0, 0), pipeline_mode=pl.Buffered(1)),
                  row_spec,
                  _mod_spec(layer, 2, 2),
                  _gain_spec(layer, D_MODEL, 2),
                  _mod_spec(layer, 3, 2),
                  _mod_spec(layer, 4, 2)],
        out_specs=[row_spec, row_spec],
        out_shape=[jax.ShapeDtypeStruct((m, D_MODEL), F32), jax.ShapeDtypeStruct((m, D_MODEL), BF16)],
        scratch_shapes=[pltpu.VMEM((k, D_MODEL), BF16)],
        compiler_params=_params(("parallel", "arbitrary"), 56),
        name=name,
    )(a, w, x, mods, gains2, mods, mods)


def _ffn_up_kernel(h_ref, wg_ref, wu_ref, a_ref, wg_s, wu_s):
    _round_weights([(wg_ref, wg_s), (wu_ref, wu_s)])
    h = h_ref[...]
    g = _dot(h, wg_s[...])
    u = _dot(h, wu_s[...])
    a_ref[...] = (g * jax.nn.sigmoid(g) * u).astype(a_ref.dtype)


def _ffn_down_kernel(row, final_norm, a_ref, wd_ref, x_ref, gate_ref, g_ref, sh_ref, sc_ref, *out_refs):
    out = x_ref[...] + gate_ref[0, row:row + 1, :] * _dot(a_ref[...], wd_ref[0])
    if final_norm:
        out_refs[0][...] = _rms(out, g_ref[0])
    else:
        out_refs[0][...] = out
        out_refs[1][...] = _normmod(out, g_ref, sh_ref, sc_ref, row)


def ffn_residual(h, x, mods, layer, row, wg, wu, wd, next_gains, final_norm, name):
    m = x.shape[0]
    tm = _tile_rows(m)
    tf = 512
    act = pl.pallas_call(
        _ffn_up_kernel,
        grid=(D_FF // tf, m // tm),
        in_specs=[pl.BlockSpec((tm, D_MODEL), lambda f, i: (i, 0)),
                  pl.BlockSpec((1, D_MODEL, tf), lambda f, i: (layer, 0, f)),
                  pl.BlockSpec((1, D_MODEL, tf), lambda f, i: (layer, 0, f))],
        out_specs=pl.BlockSpec((tm, tf), lambda f, i: (i, f)),
        out_shape=jax.ShapeDtypeStruct((m, D_FF), BF16),
        scratch_shapes=[pltpu.VMEM((D_MODEL, tf), BF16), pltpu.VMEM((D_MODEL, tf), BF16)],
        compiler_params=_params(("parallel", "arbitrary"), 48),
        name=name + "_up",
    )(h, wg, wu)
    tr = HEAD_ROWS
    nxt = 0 if final_norm else layer + 1
    row_spec = pl.BlockSpec((tr, D_MODEL), lambda i: (i, 0))
    out_shape = [jax.ShapeDtypeStruct((m, D_MODEL), F32)]
    if not final_norm:
        out_shape.append(jax.ShapeDtypeStruct((m, D_MODEL), BF16))
    outs = pl.pallas_call(
        functools.partial(_ffn_down_kernel, row, final_norm),
        grid=(m // tr,),
        in_specs=[pl.BlockSpec((tr, D_FF), lambda i: (i, 0)),
                  pl.BlockSpec((1, D_FF, D_MODEL), lambda i: (layer, 0, 0), pipeline_mode=pl.Buffered(1)),
                  row_spec,
                  _mod_spec(layer, 5, 1),
                  _gain_spec(nxt, D_MODEL, 1),
                  _mod_spec(nxt, 0, 1),
                  _mod_spec(nxt, 1, 1)],
        out_specs=[row_spec] * len(out_shape),
        out_shape=out_shape,
        compiler_params=_params(("parallel",), 52),
        name=name + "_down",
    )(act, wd, x, mods, next_gains, mods, mods)
    return (outs[0], None) if final_norm else (outs[0], outs[1])


def _shift_rows(x, k):
    n = x.shape[0]
    return pltpu.roll(x, (n - k) % n, axis=0)


def _even_mid_kernel(seq_len, zp_ref, z_ref, zn_ref, pw_ref, ps_ref, cw_ref, o_ref):
    i = pl.program_id(0)
    tm = z_ref.shape[0]
    first = i == 0
    last = i == pl.num_programs(0) - 1
    row = i * tm + lax.broadcasted_iota(jnp.int32, (tm, 1), 0)

    def with_halo(lo, hi):
        before = jnp.where(first, 0.0, zp_ref[:, lo:hi])
        after = jnp.where(last, 0.0, zn_ref[:, lo:hi])
        return jnp.concatenate([before, z_ref[:, lo:hi], after], axis=0)

    for g, w in enumerate(POOL_WINDOWS):
        lo_c, hi_c = g * POOL_GROUP, (g + 1) * POOL_GROUP
        u = with_halo(lo_c, hi_c)
        b = u
        s = 1
        while s < w:
            b = b + _shift_rows(b, s)
            s *= 2
        win = _shift_rows(b, -(w // 2))[HALO:HALO + tm]
        cnt = (jnp.minimum(row + (w - w // 2), seq_len) - jnp.maximum(row - w // 2, 0)).astype(F32)
        p = win / cnt - z_ref[:, lo_c:hi_c]
        y = _dot(p.astype(BF16), pw_ref[0, g]) * ps_ref[0, :, lo_c:hi_c]
        o_ref[:, lo_c:hi_c] = y.astype(o_ref.dtype)

    c0 = POOL_WIDTH
    gate_b = z_ref[:, c0:c0 + CONV_WIDTH]
    u = with_halo(c0 + CONV_WIDTH, c0 + 2 * CONV_WIDTH) * with_halo(c0 + 2 * CONV_WIDTH, c0 + 3 * CONV_WIDTH)
    conv = (_shift_rows(u, -1) * cw_ref[0, 0:1, :] + u * cw_ref[0, 1:2, :] + _shift_rows(u, 1) * cw_ref[0, 2:3, :])
    y_b = gate_b * conv[HALO:HALO + tm]
    o_ref[:, POOL_WIDTH:] = y_b.astype(o_ref.dtype)


def even_mid(z, pool_w, pool_scale, conv_w, li, name):
    m = z.shape[0]
    tm = min(m, 512)
    nb = tm // HALO
    last_blk = m // HALO - 1
    n_win = len(POOL_WINDOWS)
    return pl.pallas_call(
        functools.partial(_even_mid_kernel, m),
        grid=(m // tm,),
        in_specs=[pl.BlockSpec((HALO, EVEN_IN), lambda i: (jnp.maximum(i * nb - 1, 0), 0)),
                  pl.BlockSpec((tm, EVEN_IN), lambda i: (i, 0)),
                  pl.BlockSpec((HALO, EVEN_IN), lambda i: (jnp.minimum((i + 1) * nb, last_blk), 0)),
                  pl.BlockSpec((1, n_win, POOL_GROUP, POOL_GROUP), lambda i: (li, 0, 0, 0)),
                  pl.BlockSpec((1, 1, POOL_WIDTH), lambda i: (li, 0, 0)),
                  pl.BlockSpec((1, 3, CONV_WIDTH), lambda i: (li, 0, 0))],
        out_specs=pl.BlockSpec((tm, POOL_WIDTH + CONV_WIDTH), lambda i: (i, 0)),
        out_shape=jax.ShapeDtypeStruct((m, POOL_WIDTH + CONV_WIDTH), BF16),
        compiler_params=_params(("parallel",), 48),
        name=name,
    )(z, z, z, pool_w, pool_scale, conv_w)


def _rope_lanes(t, cos_ref, sin_ref):
    return t * cos_ref[...] + pltpu.roll(t, QK_ROPE, axis=1) * sin_ref[...]


def _q_up_kernel(scale, a_ref, g_ref, w_ref, cos_ref, sin_ref, qt_ref):
    n = _rms(a_ref[...], g_ref[0]).astype(BF16)
    q = _dot(n, w_ref[0])
    for h in range(MLA_HEADS):
        base = h * HEAD_PAD
        rope = _rope_lanes(q[:, base + QK_NOPE:base + HEAD_PAD], cos_ref, sin_ref)
        qt_ref[h, :QK_NOPE, :] = (q[:, base:base + QK_NOPE] * scale).T.astype(qt_ref.dtype)
        qt_ref[h, QK_NOPE:, :] = (rope * scale).T.astype(qt_ref.dtype)


def q_up(qkv_a, gains, li, w_uq, cos, sin, name):
    m = qkv_a.shape[0]
    tr = HEAD_ROWS
    scale = float((QK_NOPE + QK_ROPE) ** -0.5 * np.log2(np.e))
    return pl.pallas_call(
        functools.partial(_q_up_kernel, scale),
        grid=(m // tr,),
        in_specs=[pl.BlockSpec((tr, Q_LORA), lambda i: (i, 0)),
                  _gain_spec(li, Q_LORA, 1),
                  pl.BlockSpec((1, Q_LORA, MLA_HEADS * HEAD_PAD), lambda i: (li, 0, 0)),
                  pl.BlockSpec((tr, V7X_LANES), lambda i: (i, 0)),
                  pl.BlockSpec((tr, V7X_LANES), lambda i: (i, 0))],
        out_specs=pl.BlockSpec((MLA_HEADS, HEAD_PAD, tr), lambda i: (0, 0, i)),
        out_shape=jax.ShapeDtypeStruct((MLA_HEADS, HEAD_PAD, m), BF16),
        compiler_params=_params(("parallel",), 40),
        name=name,
    )(qkv_a, gains, w_uq, cos, sin)


def _kv_up_kernel(c_ref, r_ref, g_ref, w_ref, cos_ref, sin_ref, k_ref, vt_ref):
    n = _rms(c_ref[...], g_ref[0]).astype(BF16)
    k_rope = _rope_lanes(r_ref[...], cos_ref, sin_ref).astype(k_ref.dtype)
    kv = _dot(n, w_ref[0])
    ones = jnp.ones((VT_ROWS - V_DIM, vt_ref.shape[2]), vt_ref.dtype)
    width = QK_NOPE + V_DIM
    for h in range(MLA_HEADS):
        base = h * width
        k_ref[h, :, :QK_NOPE] = kv[:, base:base + QK_NOPE].astype(k_ref.dtype)
        k_ref[h, :, QK_NOPE:] = k_rope
        vt_ref[h, :V_DIM, :] = kv[:, base + QK_NOPE:base + width].T.astype(vt_ref.dtype)
        vt_ref[h, V_DIM:, :] = ones


def kv_up(qkv_a, gains, li, w_ukv, cos, sin, name):
    m = qkv_a.shape[0]
    tr = HEAD_ROWS
    rope_blk = (Q_LORA + KV_LORA) // V7X_LANES
    return pl.pallas_call(
        _kv_up_kernel,
        grid=(m // tr,),
        in_specs=[pl.BlockSpec((tr, KV_LORA), lambda i: (i, 1)),
                  pl.BlockSpec((tr, V7X_LANES), lambda i: (i, rope_blk)),
                  _gain_spec(li, KV_LORA, 1),
                  pl.BlockSpec((1, KV_LORA, MLA_HEADS * (QK_NOPE + V_DIM)), lambda i: (li, 0, 0)),
                  pl.BlockSpec((tr, V7X_LANES), lambda i: (i, 0)),
                  pl.BlockSpec((tr, V7X_LANES), lambda i: (i, 0))],
        out_specs=[pl.BlockSpec((MLA_HEADS, tr, HEAD_PAD), lambda i: (0, i, 0)),
                   pl.BlockSpec((MLA_HEADS, VT_ROWS, tr), lambda i: (0, 0, i))],
        out_shape=[jax.ShapeDtypeStruct((MLA_HEADS, m, HEAD_PAD), BF16),
                   jax.ShapeDtypeStruct((MLA_HEADS, VT_ROWS, m), BF16)],
        compiler_params=_params(("parallel",), 40),
        name=name,
    )(qkv_a, qkv_a, gains, w_ukv, cos, sin)


def _flash_kernel(piece, n_lat, has_ctx, *refs):
    if has_ctx:
        qt_ref, k_ref, vt_ref, kc_ref, vtc_ref, o_ref = refs
    else:
        qt_ref, k_ref, vt_ref, o_ref = refs
    qt = qt_ref[0]
    tq = qt.shape[1]
    n_all = n_lat + int(has_ctx)

    def keys(i):
        return kc_ref[0] if i >= n_lat else k_ref[0, i * piece:(i + 1) * piece, :]

    def values_t(i):
        return vtc_ref[0] if i >= n_lat else vt_ref[0, :, i * piece:(i + 1) * piece]

    m = jnp.full((1, tq), -jnp.inf, F32)
    acc = jnp.zeros((VT_ROWS, tq), F32)
    pending = [_dot(keys(i), qt) for i in range(min(ATTN_AHEAD, n_all))]
    for t in range(n_all):
        s = pending.pop(0)
        if t + ATTN_AHEAD < n_all:
            pending.append(_dot(keys(t + ATTN_AHEAD), qt))
        m_new = jnp.maximum(m, jnp.max(s, axis=0, keepdims=True))
        p = jnp.exp2(s - m_new).astype(BF16)
        acc = jnp.exp2(m - m_new) * acc + _dot(values_t(t), p)
        m = m_new
    o_ref[...] = (acc[:V_DIM] / acc[V_DIM:V_DIM + 1]).T.astype(o_ref.dtype)


def flash_attention(qt, k, vt, k_ctx, vt_ctx, name):
    h, _, m = qt.shape
    n_k = k.shape[1]
    tq = min(m, ATTN_TQ)
    piece = min(n_k, ATTN_PIECE)
    has_ctx = k_ctx is not None
    in_specs = [pl.BlockSpec((1, HEAD_PAD, tq), lambda hh, i: (hh, 0, i)),
                pl.BlockSpec((1, n_k, HEAD_PAD), lambda hh, i: (hh, 0, 0)),
                pl.BlockSpec((1, VT_ROWS, n_k), lambda hh, i: (hh, 0, 0))]
    args = [qt, k, vt]
    if has_ctx:
        n_c = k_ctx.shape[1]
        in_specs += [pl.BlockSpec((1, n_c, HEAD_PAD), lambda hh, i: (hh, 0, 0)),
                     pl.BlockSpec((1, VT_ROWS, n_c), lambda hh, i: (hh, 0, 0))]
        args += [k_ctx, vt_ctx]
    return pl.pallas_call(
        functools.partial(_flash_kernel, piece, n_k // piece, has_ctx),
        grid=(h, m // tq),
        in_specs=in_specs,
        out_specs=pl.BlockSpec((tq, V_DIM), lambda hh, i: (i, hh)),
        out_shape=jax.ShapeDtypeStruct((m, h * V_DIM), BF16),
        compiler_params=_params(("parallel", "parallel"), 48),
        name=name,
    )(*args)


def _swap_pairs(w):
    q = QK_ROPE // 4
    r1, r2, c1, c2 = (w[..., j * q:(j + 1) * q] for j in range(4))
    return jnp.concatenate([-r2, r1, -c2, c1], axis=-1)


def _rope_tables(n):
    q = QK_ROPE // 4
    rows = n // GRID_W
    inv = jnp.power(jnp.float32(ROPE_THETA), -jnp.arange(q, dtype=F32) / q)
    ar = jnp.arange(rows).astype(F32)[:, None] * inv
    ac = jnp.arange(GRID_W).astype(F32)[:, None] * inv
    cos_r, sin_r = (jnp.repeat(f(ar), GRID_W, axis=0) for f in (jnp.cos, jnp.sin))
    cos_c, sin_c = (jnp.tile(f(ac), (rows, 1)) for f in (jnp.cos, jnp.sin))
    zeros = jnp.zeros((n, V7X_LANES - QK_ROPE), F32)
    cos = jnp.concatenate([cos_r, cos_r, cos_c, cos_c, zeros], axis=1)
    sin = jnp.concatenate([sin_r, sin_r, sin_c, sin_c, zeros], axis=1)
    return cos, sin


def _identity_tables(n):
    cos = jnp.concatenate([jnp.ones((n, QK_ROPE), F32), jnp.zeros((n, V7X_LANES - QK_ROPE), F32)], axis=1)
    return cos, jnp.zeros((n, V7X_LANES), F32)


def _mla_weights(w_dq, w_uq, w_dkv):
    n = w_dq.shape[0]
    rope = w_dkv[..., KV_LORA:]
    w_down = jnp.concatenate([w_dq, w_dkv[..., :KV_LORA], rope, _swap_pairs(rope)], axis=-1)
    wq = w_uq.reshape(n, Q_LORA, MLA_HEADS, QK_NOPE + QK_ROPE)
    w_up_q = jnp.concatenate([wq, _swap_pairs(wq[..., QK_NOPE:])], axis=-1)
    return w_down, w_up_q.reshape(n, Q_LORA, MLA_HEADS * HEAD_PAD).astype(BF16)


def kernel(x, c, ctx, c_ctx, ada_w, ada_b, norm1_g, norm2_g, even_w_in, pool_w, pool_scale, conv_w,
           even_w_out, mla_w_dq, mla_q_norm_g, mla_w_uq, mla_w_dkv, mla_kv_norm_g, mla_w_ukv, mla_w_o,
           ffn_w_gate, ffn_w_up, ffn_w_down, final_norm_g):
    n_lat, n_ctx = x.shape[1], ctx.shape[1]
    x_lat, x_ctx = x[0], ctx[0]

    cond8 = jnp.concatenate([c, c_ctx[None, :], jnp.zeros((V7X_SUBLANES - 2, D_MODEL), F32)], axis=0)
    mods = ada_table(cond8, ada_w, ada_b)
    g1 = norm1_g.reshape(DEPTH, 1, D_MODEL)
    g2 = norm2_g.reshape(DEPTH, 1, D_MODEL)
    gq = mla_q_norm_g.reshape(-1, 1, Q_LORA)
    gkv = mla_kv_norm_g.reshape(-1, 1, KV_LORA)
    final_g = final_norm_g.reshape(1, 1, D_MODEL)
    rope_lat = _rope_tables(n_lat)
    rope_ctx = _identity_tables(n_ctx)

    w_in, pw = even_w_in.astype(BF16), pool_w.astype(BF16)
    ps = pool_scale.reshape(-1, 1, POOL_WIDTH)
    w_down, w_up_q = _mla_weights(mla_w_dq, mla_w_uq, mla_w_dkv)
    w_ukv = mla_w_ukv.astype(BF16)
    wd = ffn_w_down.astype(BF16)

    state = {"lat": (x_lat, normmod(x_lat, g1, mods, 0, LAT_ROW, 0, "norm_in_lat")),
             "ctx": (x_ctx, normmod(x_ctx, g1, mods, 0, CTX_ROW, 0, "norm_in_ctx"))}
    rows = {"lat": LAT_ROW, "ctx": CTX_ROW}

    for layer in range(DEPTH):
        last = layer == DEPTH - 1
        odd = layer % 2 == 1
        li = layer // 2
        tags = ["lat"] + (["ctx"] if odd or not last else [])
        branch = {}

        if odd:
            proj = {}
            for tag in tags:
                cos, sin = rope_lat if tag == "lat" else rope_ctx
                a = wcast_matmul(state[tag][1], w_down, li, f"mla_down_{tag}")
                k, vt = kv_up(a, gkv, li, w_ukv, cos, sin, f"kv_up_{tag}")
                need_q = tag == "lat" or not last
                qt = q_up(a, gq, li, w_up_q, cos, sin, f"q_up_{tag}") if need_q else None
                proj[tag] = (qt, k, vt)
            qt, k, vt = proj["lat"]
            qtc, kc, vtc = proj["ctx"]
            branch["lat"] = flash_attention(qt, k, vt, kc, vtc, "attn_lat")
            if not last:
                branch["ctx"] = flash_attention(qtc, kc, vtc, None, None, "attn_ctx")
            w_proj, proj_name = mla_w_o, "attn_out"
        else:
            for tag in tags:
                z = matmul_cols(state[tag][1], w_in, li, 512, f"even_in_{tag}")
                branch[tag] = even_mid(z, pw, ps, conv_w, li, f"even_mid_{tag}")
            w_proj, proj_name = even_w_out, "even_out"

        for tag, y in branch.items():
            x1, h2 = matmul_residual(y, w_proj, li, state[tag][0], mods, layer, rows[tag], g2,
                                     f"{proj_name}_{tag}")
            final = last and tag == "lat"
            state[tag] = ffn_residual(h2, x1, mods, layer, rows[tag], ffn_w_gate, ffn_w_up, wd,
                                      final_g if final else g1, final, f"ffn_{tag}")

    return state["lat"][0][None]
```

```python
import functools

import jax
import jax.numpy as jnp
import numpy as np
from jax import lax
from jax.experimental import pallas as pl
from jax.experimental.pallas import tpu as pltpu

D_MODEL = 2048
DEPTH = 4
GRID_W = 64
EPS = 1e-6
POOL_WINDOWS = (2, 4, 8, 16)
POOL_WIDTH = 1024
POOL_GROUP = 256
CONV_WIDTH = 1024
EVEN_IN = POOL_WIDTH + 3 * CONV_WIDTH
MLA_HEADS = 16
QK_NOPE = 128
QK_ROPE = 64
V_DIM = 128
Q_LORA = 512
KV_LORA = 512
ROPE_THETA = 10000.0
D_FF = 5632

V7X_LANES = 128
V7X_SUBLANES = 8
V7X_MXU_DIM = 256

HEAD_PAD = V7X_MXU_DIM
HALO = V7X_SUBLANES
LAT_ROW, CTX_ROW = 0, 1
BF16_SUBLANES = 2 * V7X_SUBLANES
VT_ROWS = V_DIM + BF16_SUBLANES
ATTN_TQ = 1024
ATTN_PIECE = 2 * V7X_MXU_DIM
ATTN_AHEAD = 2
HEAD_ROWS = 256
BF16 = jnp.bfloat16
F32 = jnp.float32


def _params(semantics, vmem_mb):
    return pltpu.CompilerParams(dimension_semantics=semantics,
                                vmem_limit_bytes=vmem_mb * 1024 * 1024)


def _tile_rows(m):
    return min(m, 1024)


def _dot(a, b):
    return jnp.dot(a, b, preferred_element_type=F32)


def _rms(x, g):
    return x * lax.rsqrt(jnp.mean(x * x, axis=-1, keepdims=True) + EPS) * g


def _normmod(x, g_ref, sh_ref, sc_ref, row):
    inv = lax.rsqrt(jnp.mean(x * x, axis=-1, keepdims=True) + EPS)
    gain = g_ref[0] * (1.0 + sc_ref[0, row:row + 1, :])
    return (x * inv * gain + sh_ref[0, row:row + 1, :]).astype(BF16)


def _mod_spec(layer, chunk, ngrid):
    if ngrid == 1:
        return pl.BlockSpec((1, V7X_SUBLANES, D_MODEL), lambda i: (layer, 0, chunk))
    return pl.BlockSpec((1, V7X_SUBLANES, D_MODEL), lambda i, j: (layer, 0, chunk))


def _gain_spec(layer, width, ngrid):
    if ngrid == 1:
        return pl.BlockSpec((1, 1, width), lambda i: (layer, 0, 0))
    return pl.BlockSpec((1, 1, width), lambda i, j: (layer, 0, 0))


def _ada_kernel(cond_ref, w_ref, b_ref, o_ref):
    c = cond_ref[...]
    s = (c * jax.nn.sigmoid(c)).astype(BF16)
    o_ref[0] = _dot(s, w_ref[0].astype(BF16)) + b_ref[0]


def ada_table(cond8, ada_w, ada_b):
    tn = 1024
    n_out = 6 * D_MODEL
    return pl.pallas_call(
        _ada_kernel,
        grid=(DEPTH, n_out // tn),
        in_specs=[pl.BlockSpec((V7X_SUBLANES, D_MODEL), lambda l, j: (0, 0)),
                  pl.BlockSpec((1, D_MODEL, tn), lambda l, j: (l, 0, j)),
                  pl.BlockSpec((1, 1, tn), lambda l, j: (l, 0, j))],
        out_specs=pl.BlockSpec((1, V7X_SUBLANES, tn), lambda l, j: (l, 0, j)),
        out_shape=jax.ShapeDtypeStruct((DEPTH, V7X_SUBLANES, n_out), F32),
        compiler_params=_params(("parallel", "parallel"), 40),
        name="ada_table",
    )(cond8, ada_w, ada_b.reshape(DEPTH, 1, n_out))


def _matmul_cols_kernel(h_ref, w_ref, o_ref):
    o_ref[...] = _dot(h_ref[...], w_ref[0])


def matmul_cols(h, w, li, tn, name):
    m, k = h.shape
    tm = _tile_rows(m)
    n_out = w.shape[2]
    return pl.pallas_call(
        _matmul_cols_kernel,
        grid=(m // tm, n_out // tn),
        in_specs=[pl.BlockSpec((tm, k), lambda i, j: (i, 0)),
                  pl.BlockSpec((1, k, tn), lambda i, j: (li, 0, j))],
        out_specs=pl.BlockSpec((tm, tn), lambda i, j: (i, j)),
        out_shape=jax.ShapeDtypeStruct((m, n_out), F32),
        compiler_params=_params(("parallel", "parallel"), 40),
        name=name,
    )(h, w)


def _normmod_kernel(row, x_ref, g_ref, sh_ref, sc_ref, h_ref):
    h_ref[...] = _normmod(x_ref[...], g_ref, sh_ref, sc_ref, row)


def normmod(x, gains, mods, layer, row, chunk, name):
    m = x.shape[0]
    tm = min(m, 512)
    return pl.pallas_call(
        functools.partial(_normmod_kernel, row),
        grid=(m // tm,),
        in_specs=[pl.BlockSpec((tm, D_MODEL), lambda i: (i, 0)),
                  _gain_spec(layer, D_MODEL, 1),
                  _mod_spec(layer, chunk, 1),
                  _mod_spec(layer, chunk + 1, 1)],
        out_specs=pl.BlockSpec((tm, D_MODEL), lambda i: (i, 0)),
        out_shape=jax.ShapeDtypeStruct((m, D_MODEL), BF16),
        compiler_params=_params(("parallel",), 32),
        name=name,
    )(x, gains, mods, mods)


def _round_weights(pairs):
    @pl.when(pl.program_id(1) == 0)
    def _():
        for w_ref, w_s in pairs:
            w_s[...] = w_ref[0].astype(BF16)


def _wcast_matmul_kernel(h_ref, w_ref, o_ref, w_s):
    _round_weights([(w_ref, w_s)])
    o_ref[...] = _dot(h_ref[...], w_s[...])


def wcast_matmul(h, w, li, name):
    m, k = h.shape
    n_out = w.shape[2]
    tm = _tile_rows(m)
    return pl.pallas_call(
        _wcast_matmul_kernel,
        grid=(1, m // tm),
        in_specs=[pl.BlockSpec((tm, k), lambda j, i: (i, 0)),
                  pl.BlockSpec((1, k, n_out), lambda j, i: (li, 0, 0), pipeline_mode=pl.Buffered(1))],
        out_specs=pl.BlockSpec((tm, n_out), lambda j, i: (i, 0)),
        out_shape=jax.ShapeDtypeStruct((m, n_out), F32),
        scratch_shapes=[pltpu.VMEM((k, n_out), BF16)],
        compiler_params=_params(("parallel", "arbitrary"), 48),
        name=name,
    )(h, w)


def _matmul_residual_kernel(row, a_ref, w_ref, x_ref, gate_ref, g_ref, sh_ref, sc_ref, o_ref, h_ref, w_s):
    _round_weights([(w_ref, w_s)])
    out = x_ref[...] + gate_ref[0, row:row + 1, :] * _dot(a_ref[...], w_s[...])
    o_ref[...] = out
    h_ref[...] = _normmod(out, g_ref, sh_ref, sc_ref, row)


def matmul_residual(a, w, li, x, mods, layer, row, gains2, name):
    m, k = a.shape
    tm = min(m, 512)
    row_spec = pl.BlockSpec((tm, D_MODEL), lambda j, i: (i, 0))
    return pl.pallas_call(
        functools.partial(_matmul_residual_kernel, row),
        grid=(1, m // tm),
        in_specs=[pl.BlockSpec((tm, k), lambda j, i: (i, 0)),
                  pl.BlockSpec((1, k, D_MODEL), lambda j, i: (li, 0, 0), pipeline_mode=pl.Buffered(1)),
                  row_spec,
                  _mod_spec(layer, 2, 2),
                  _gain_spec(layer, D_MODEL, 2),
                  _mod_spec(layer, 3, 2),
                  _mod_spec(layer, 4, 2)],
        out_specs=[row_spec, row_spec],
        out_shape=[jax.ShapeDtypeStruct((m, D_MODEL), F32), jax.ShapeDtypeStruct((m, D_MODEL), BF16)],
        scratch_shapes=[pltpu.VMEM((k, D_MODEL), BF16)],
        compiler_params=_params(("parallel", "arbitrary"), 56),
        name=name,
    )(a, w, x, mods, gains2, mods, mods)


def _ffn_up_kernel(n_lat, with_ctx, *refs):
    if with_ctx:
        h_ref, wg_ref, wu_ref, hc_ref, a_ref, ac_ref, wg_s, wu_s = refs
    else:
        h_ref, wg_ref, wu_ref, a_ref, wg_s, wu_s = refs
    _round_weights([(wg_ref, wg_s), (wu_ref, wu_s)])

    def swiglu(src_ref, dst_ref):
        h = src_ref[...]
        g = _dot(h, wg_s[...])
        u = _dot(h, wu_s[...])
        dst_ref[...] = (g * jax.nn.sigmoid(g) * u).astype(dst_ref.dtype)

    if with_ctx:
        i = pl.program_id(1)
        pl.when(i < n_lat)(lambda: swiglu(h_ref, a_ref))
        pl.when(i == n_lat)(lambda: swiglu(hc_ref, ac_ref))
    else:
        swiglu(h_ref, a_ref)


def _ffn_down_kernel(n_lat, final_norm, with_ctx, a_ref, wd_ref, x_ref, gate_ref, g_ref, sh_ref, sc_ref, *rest):
    def rows(a, x, row, o_ref, h_ref):
        out = x[...] + gate_ref[0, row:row + 1, :] * _dot(a[...], wd_ref[0])
        if final_norm:
            o_ref[...] = _rms(out, g_ref[0])
        else:
            o_ref[...] = out
            h_ref[...] = _normmod(out, g_ref, sh_ref, sc_ref, row)

    if with_ctx:
        ac_ref, xc_ref, o_ref, h_ref, oc_ref, hc_ref = rest
        i = pl.program_id(0)
        pl.when(i < n_lat)(lambda: rows(a_ref, x_ref, LAT_ROW, o_ref, h_ref))
        pl.when(i == n_lat)(lambda: rows(ac_ref, xc_ref, CTX_ROW, oc_ref, hc_ref))
    else:
        rows(a_ref, x_ref, LAT_ROW, rest[0], None if final_norm else rest[1])


def ffn_residual(lat, ctx, mods, layer, wg, wu, wd, next_gains, final_norm, name):
    h, x = lat
    m = x.shape[0]
    tm = _tile_rows(m)
    tf = 512
    with_ctx = ctx is not None
    n_up = m // tm
    last_up = n_up - 1
    in_specs = [pl.BlockSpec((tm, D_MODEL), lambda f, i: (jnp.minimum(i, last_up), 0)),
                pl.BlockSpec((1, D_MODEL, tf), lambda f, i: (layer, 0, f)),
                pl.BlockSpec((1, D_MODEL, tf), lambda f, i: (layer, 0, f))]
    out_specs = [pl.BlockSpec((tm, tf), lambda f, i: (jnp.minimum(i, last_up), f))]
    out_shape = [jax.ShapeDtypeStruct((m, D_FF), BF16)]
    args = [h, wg, wu]
    if with_ctx:
        hc, xc = ctx
        mc = xc.shape[0]
        in_specs.append(pl.BlockSpec((mc, D_MODEL), lambda f, i: (0, 0)))
        out_specs.append(pl.BlockSpec((mc, tf), lambda f, i: (0, f)))
        out_shape.append(jax.ShapeDtypeStruct((mc, D_FF), BF16))
        args.append(hc)
    acts = pl.pallas_call(
        functools.partial(_ffn_up_kernel, n_up, with_ctx),
        grid=(D_FF // tf, n_up + int(with_ctx)),
        in_specs=in_specs,
        out_specs=out_specs,
        out_shape=out_shape,
        scratch_shapes=[pltpu.VMEM((D_MODEL, tf), BF16), pltpu.VMEM((D_MODEL, tf), BF16)],
        compiler_params=_params(("parallel", "arbitrary"), 48),
        name=name + "_up",
    )(*args)

    tr = HEAD_ROWS
    n_dn = m // tr
    last_dn = n_dn - 1
    nxt = 0 if final_norm else layer + 1
    row_spec = pl.BlockSpec((tr, D_MODEL), lambda i: (jnp.minimum(i, last_dn), 0))
    in_specs = [pl.BlockSpec((tr, D_FF), lambda i: (jnp.minimum(i, last_dn), 0)),
                pl.BlockSpec((1, D_FF, D_MODEL), lambda i: (layer, 0, 0), pipeline_mode=pl.Buffered(1)),
                row_spec,
                _mod_spec(layer, 5, 1),
                _gain_spec(nxt, D_MODEL, 1),
                _mod_spec(nxt, 0, 1),
                _mod_spec(nxt, 1, 1)]
    out_specs = [row_spec]
    out_shape = [jax.ShapeDtypeStruct((m, D_MODEL), F32)]
    args = [acts[0], wd, x, mods, next_gains, mods, mods]
    if not final_norm:
        out_specs.append(row_spec)
        out_shape.append(jax.ShapeDtypeStruct((m, D_MODEL), BF16))
    if with_ctx:
        whole = lambda cols: pl.BlockSpec((mc, cols), lambda i: (0, 0), pipeline_mode=pl.Buffered(1))
        in_specs += [whole(D_FF), whole(D_MODEL)]
        args += [acts[1], xc]
        out_specs += [pl.BlockSpec((mc, D_MODEL), lambda i: (0, 0))] * 2
        out_shape += [jax.ShapeDtypeStruct((mc, D_MODEL), F32), jax.ShapeDtypeStruct((mc, D_MODEL), BF16)]
    outs = pl.pallas_call(
        functools.partial(_ffn_down_kernel, n_dn, final_norm, with_ctx),
        grid=(n_dn + int(with_ctx),),
        in_specs=in_specs,
        out_specs=out_specs,
        out_shape=out_shape,
        compiler_params=_params(("arbitrary",), 56),
        name=name + "_down",
    )(*args)
    if final_norm:
        return {"lat": (outs[0], None)}
    state = {"lat": (outs[0], outs[1])}
    if with_ctx:
        state["ctx"] = (outs[2], outs[3])
    return state


def _shift_rows(x, k):
    n = x.shape[0]
    return pltpu.roll(x, (n - k) % n, axis=0)


def _even_mid_kernel(seq_len, zp_ref, z_ref, zn_ref, pw_ref, ps_ref, cw_ref, o_ref):
    i = pl.program_id(0)
    tm = z_ref.shape[0]
    first = i == 0
    last = i == pl.num_programs(0) - 1
    row = i * tm + lax.broadcasted_iota(jnp.int32, (tm, 1), 0)

    def with_halo(lo, hi):
        before = jnp.where(first, 0.0, zp_ref[:, lo:hi])
        after = jnp.where(last, 0.0, zn_ref[:, lo:hi])
        return jnp.concatenate([before, z_ref[:, lo:hi], after], axis=0)

    for g, w in enumerate(POOL_WINDOWS):
        lo_c, hi_c = g * POOL_GROUP, (g + 1) * POOL_GROUP
        u = with_halo(lo_c, hi_c)
        b = u
        s = 1
        while s < w:
            b = b + _shift_rows(b, s)
            s *= 2
        win = _shift_rows(b, -(w // 2))[HALO:HALO + tm]
        cnt = (jnp.minimum(row + (w - w // 2), seq_len) - jnp.maximum(row - w // 2, 0)).astype(F32)
        p = win / cnt - z_ref[:, lo_c:hi_c]
        y = _dot(p.astype(BF16), pw_ref[0, g]) * ps_ref[0, :, lo_c:hi_c]
        o_ref[:, lo_c:hi_c] = y.astype(o_ref.dtype)

    c0 = POOL_WIDTH
    gate_b = z_ref[:, c0:c0 + CONV_WIDTH]
    u = with_halo(c0 + CONV_WIDTH, c0 + 2 * CONV_WIDTH) * with_halo(c0 + 2 * CONV_WIDTH, c0 + 3 * CONV_WIDTH)
    conv = (_shift_rows(u, -1) * cw_ref[0, 0:1, :] + u * cw_ref[0, 1:2, :] + _shift_rows(u, 1) * cw_ref[0, 2:3, :])
    y_b = gate_b * conv[HALO:HALO + tm]
    o_ref[:, POOL_WIDTH:] = y_b.astype(o_ref.dtype)


def even_mid(z, pool_w, pool_scale, conv_w, li, name):
    m = z.shape[0]
    tm = min(m, 512)
    nb = tm // HALO
    last_blk = m // HALO - 1
    n_win = len(POOL_WINDOWS)
    return pl.pallas_call(
        functools.partial(_even_mid_kernel, m),
        grid=(m // tm,),
        in_specs=[pl.BlockSpec((HALO, EVEN_IN), lambda i: (jnp.maximum(i * nb - 1, 0), 0)),
                  pl.BlockSpec((tm, EVEN_IN), lambda i: (i, 0)),
                  pl.BlockSpec((HALO, EVEN_IN), lambda i: (jnp.minimum((i + 1) * nb, last_blk), 0)),
                  pl.BlockSpec((1, n_win, POOL_GROUP, POOL_GROUP), lambda i: (li, 0, 0, 0)),
                  pl.BlockSpec((1, 1, POOL_WIDTH), lambda i: (li, 0, 0)),
                  pl.BlockSpec((1, 3, CONV_WIDTH), lambda i: (li, 0, 0))],
        out_specs=pl.BlockSpec((tm, POOL_WIDTH + CONV_WIDTH), lambda i: (i, 0)),
        out_shape=jax.ShapeDtypeStruct((m, POOL_WIDTH + CONV_WIDTH), BF16),
        compiler_params=_params(("parallel",), 48),
        name=name,
    )(z, z, z, pool_w, pool_scale, conv_w)


def _rope_lanes(t, cos_ref, sin_ref):
    return t * cos_ref[...] + pltpu.roll(t, QK_ROPE, axis=1) * sin_ref[...]


def _q_up_kernel(scale, a_ref, g_ref, w_ref, cos_ref, sin_ref, qt_ref):
    n = _rms(a_ref[...], g_ref[0]).astype(BF16)
    q = _dot(n, w_ref[0])
    for h in range(MLA_HEADS):
        base = h * HEAD_PAD
        rope = _rope_lanes(q[:, base + QK_NOPE:base + HEAD_PAD], cos_ref, sin_ref)
        qt_ref[h, :QK_NOPE, :] = (q[:, base:base + QK_NOPE] * scale).T.astype(qt_ref.dtype)
        qt_ref[h, QK_NOPE:, :] = (rope * scale).T.astype(qt_ref.dtype)


def q_up(qkv_a, gains, li, w_uq, cos, sin, name):
    m = qkv_a.shape[0]
    tr = HEAD_ROWS
    scale = float((QK_NOPE + QK_ROPE) ** -0.5 * np.log2(np.e))
    return pl.pallas_call(
        functools.partial(_q_up_kernel, scale),
        grid=(m // tr,),
        in_specs=[pl.BlockSpec((tr, Q_LORA), lambda i: (i, 0)),
                  _gain_spec(li, Q_LORA, 1),
                  pl.BlockSpec((1, Q_LORA, MLA_HEADS * HEAD_PAD), lambda i: (li, 0, 0)),
                  pl.BlockSpec((tr, V7X_LANES), lambda i: (i, 0)),
                  pl.BlockSpec((tr, V7X_LANES), lambda i: (i, 0))],
        out_specs=pl.BlockSpec((MLA_HEADS, HEAD_PAD, tr), lambda i: (0, 0, i)),
        out_shape=jax.ShapeDtypeStruct((MLA_HEADS, HEAD_PAD, m), BF16),
        compiler_params=_params(("parallel",), 40),
        name=name,
    )(qkv_a, gains, w_uq, cos, sin)


def _kv_up_kernel(c_ref, r_ref, g_ref, w_ref, cos_ref, sin_ref, k_ref, vt_ref):
    n = _rms(c_ref[...], g_ref[0]).astype(BF16)
    k_rope = _rope_lanes(r_ref[...], cos_ref, sin_ref).astype(k_ref.dtype)
    kv = _dot(n, w_ref[0])
    ones = jnp.ones((VT_ROWS - V_DIM, vt_ref.shape[2]), vt_ref.dtype)
    width = QK_NOPE + V_DIM
    for h in range(MLA_HEADS):
        base = h * width
        k_ref[h, :, :QK_NOPE] = kv[:, base:base + QK_NOPE].astype(k_ref.dtype)
        k_ref[h, :, QK_NOPE:] = k_rope
        vt_ref[h, :V_DIM, :] = kv[:, base + QK_NOPE:base + width].T.astype(vt_ref.dtype)
        vt_ref[h, V_DIM:, :] = ones


def kv_up(qkv_a, gains, li, w_ukv, cos, sin, name):
    m = qkv_a.shape[0]
    tr = HEAD_ROWS
    rope_blk = (Q_LORA + KV_LORA) // V7X_LANES
    return pl.pallas_call(
        _kv_up_kernel,
        grid=(m // tr,),
        in_specs=[pl.BlockSpec((tr, KV_LORA), lambda i: (i, 1)),
                  pl.BlockSpec((tr, V7X_LANES), lambda i: (i, rope_blk)),
                  _gain_spec(li, KV_LORA, 1),
                  pl.BlockSpec((1, KV_LORA, MLA_HEADS * (QK_NOPE + V_DIM)), lambda i: (li, 0, 0)),
                  pl.BlockSpec((tr, V7X_LANES), lambda i: (i, 0)),
                  pl.BlockSpec((tr, V7X_LANES), lambda i: (i, 0))],
        out_specs=[pl.BlockSpec((MLA_HEADS, tr, HEAD_PAD), lambda i: (0, i, 0)),
                   pl.BlockSpec((MLA_HEADS, VT_ROWS, tr), lambda i: (0, 0, i))],
        out_shape=[jax.ShapeDtypeStruct((MLA_HEADS, m, HEAD_PAD), BF16),
                   jax.ShapeDtypeStruct((MLA_HEADS, VT_ROWS, m), BF16)],
        compiler_params=_params(("parallel",), 40),
        name=name,
    )(qkv_a, qkv_a, gains, w_ukv, cos, sin)


def _flash_kernel(piece, n_lat, has_ctx, *refs):
    if has_ctx:
        qt_ref, k_ref, vt_ref, kc_ref, vtc_ref, o_ref = refs
    else:
        qt_ref, k_ref, vt_ref, o_ref = refs
    qt = qt_ref[0]
    tq = qt.shape[1]
    n_all = n_lat + int(has_ctx)

    def keys(i):
        return kc_ref[0] if i >= n_lat else k_ref[0, i * piece:(i + 1) * piece, :]

    def values_t(i):
        return vtc_ref[0] if i >= n_lat else vt_ref[0, :, i * piece:(i + 1) * piece]

    m = jnp.full((1, tq), -jnp.inf, F32)
    acc = jnp.zeros((VT_ROWS, tq), F32)
    pending = [_dot(keys(i), qt) for i in range(min(ATTN_AHEAD, n_all))]
    for t in range(n_all):
        s = pending.pop(0)
        if t + ATTN_AHEAD < n_all:
            pending.append(_dot(keys(t + ATTN_AHEAD), qt))
        m_new = jnp.maximum(m, jnp.max(s, axis=0, keepdims=True))
        p = jnp.exp2(s - m_new).astype(BF16)
        acc = jnp.exp2(m - m_new) * acc + _dot(values_t(t), p)
        m = m_new
    o_ref[...] = (acc[:V_DIM] / acc[V_DIM:V_DIM + 1]).T.astype(o_ref.dtype)


def flash_attention(qt, k, vt, k_ctx, vt_ctx, name):
    h, _, m = qt.shape
    n_k = k.shape[1]
    tq = min(m, ATTN_TQ)
    piece = min(n_k, ATTN_PIECE)
    has_ctx = k_ctx is not None
    in_specs = [pl.BlockSpec((1, HEAD_PAD, tq), lambda hh, i: (hh, 0, i)),
                pl.BlockSpec((1, n_k, HEAD_PAD), lambda hh, i: (hh, 0, 0)),
                pl.BlockSpec((1, VT_ROWS, n_k), lambda hh, i: (hh, 0, 0))]
    args = [qt, k, vt]
    if has_ctx:
        n_c = k_ctx.shape[1]
        in_specs += [pl.BlockSpec((1, n_c, HEAD_PAD), lambda hh, i: (hh, 0, 0)),
                     pl.BlockSpec((1, VT_ROWS, n_c), lambda hh, i: (hh, 0, 0))]
        args += [k_ctx, vt_ctx]
    return pl.pallas_call(
        functools.partial(_flash_kernel, piece, n_k // piece, has_ctx),
        grid=(h, m // tq),
        in_specs=in_specs,
        out_specs=pl.BlockSpec((tq, V_DIM), lambda hh, i: (i, hh)),
        out_shape=jax.ShapeDtypeStruct((m, h * V_DIM), BF16),
        compiler_params=_params(("parallel", "parallel"), 48),
        name=name,
    )(*args)


def _swap_pairs(w):
    q = QK_ROPE // 4
    r1, r2, c1, c2 = (w[..., j * q:(j + 1) * q] for j in range(4))
    return jnp.concatenate([-r2, r1, -c2, c1], axis=-1)


def _rope_tables(n):
    q = QK_ROPE // 4
    rows = n // GRID_W
    inv = jnp.power(jnp.float32(ROPE_THETA), -jnp.arange(q, dtype=F32) / q)
    ar = jnp.arange(rows).astype(F32)[:, None] * inv
    ac = jnp.arange(GRID_W).astype(F32)[:, None] * inv
    cos_r, sin_r = (jnp.repeat(f(ar), GRID_W, axis=0) for f in (jnp.cos, jnp.sin))
    cos_c, sin_c = (jnp.tile(f(ac), (rows, 1)) for f in (jnp.cos, jnp.sin))
    zeros = jnp.zeros((n, V7X_LANES - QK_ROPE), F32)
    cos = jnp.concatenate([cos_r, cos_r, cos_c, cos_c, zeros], axis=1)
    sin = jnp.concatenate([sin_r, sin_r, sin_c, sin_c, zeros], axis=1)
    return cos, sin


def _identity_tables(n):
    cos = jnp.concatenate([jnp.ones((n, QK_ROPE), F32), jnp.zeros((n, V7X_LANES - QK_ROPE), F32)], axis=1)
    return cos, jnp.zeros((n, V7X_LANES), F32)


def _mla_weights(w_dq, w_uq, w_dkv):
    n = w_dq.shape[0]
    rope = w_dkv[..., KV_LORA:]
    w_down = jnp.concatenate([w_dq, w_dkv[..., :KV_LORA], rope, _swap_pairs(rope)], axis=-1)
    wq = w_uq.reshape(n, Q_LORA, MLA_HEADS, QK_NOPE + QK_ROPE)
    w_up_q = jnp.concatenate([wq, _swap_pairs(wq[..., QK_NOPE:])], axis=-1)
    return w_down, w_up_q.reshape(n, Q_LORA, MLA_HEADS * HEAD_PAD).astype(BF16)


def kernel(x, c, ctx, c_ctx, ada_w, ada_b, norm1_g, norm2_g, even_w_in, pool_w, pool_scale, conv_w,
           even_w_out, mla_w_dq, mla_q_norm_g, mla_w_uq, mla_w_dkv, mla_kv_norm_g, mla_w_ukv, mla_w_o,
           ffn_w_gate, ffn_w_up, ffn_w_down, final_norm_g):
    n_lat, n_ctx = x.shape[1], ctx.shape[1]
    x_lat, x_ctx = x[0], ctx[0]

    cond8 = jnp.concatenate([c, c_ctx[None, :], jnp.zeros((V7X_SUBLANES - 2, D_MODEL), F32)], axis=0)
    mods = ada_table(cond8, ada_w, ada_b)
    g1 = norm1_g.reshape(DEPTH, 1, D_MODEL)
    g2 = norm2_g.reshape(DEPTH, 1, D_MODEL)
    gq = mla_q_norm_g.reshape(-1, 1, Q_LORA)
    gkv = mla_kv_norm_g.reshape(-1, 1, KV_LORA)
    final_g = final_norm_g.reshape(1, 1, D_MODEL)
    rope_lat = _rope_tables(n_lat)
    rope_ctx = _identity_tables(n_ctx)

    w_in, pw = even_w_in.astype(BF16), pool_w.astype(BF16)
    ps = pool_scale.reshape(-1, 1, POOL_WIDTH)
    w_down, w_up_q = _mla_weights(mla_w_dq, mla_w_uq, mla_w_dkv)
    w_ukv = mla_w_ukv.astype(BF16)
    wd = ffn_w_down.astype(BF16)

    state = {"lat": (x_lat, normmod(x_lat, g1, mods, 0, LAT_ROW, 0, "norm_in_lat")),
             "ctx": (x_ctx, normmod(x_ctx, g1, mods, 0, CTX_ROW, 0, "norm_in_ctx"))}
    rows = {"lat": LAT_ROW, "ctx": CTX_ROW}

    for layer in range(DEPTH):
        last = layer == DEPTH - 1
        odd = layer % 2 == 1
        li = layer // 2
        tags = ["lat"] + (["ctx"] if odd or not last else [])
        branch = {}

        if odd:
            proj = {}
            for tag in tags:
                cos, sin = rope_lat if tag == "lat" else rope_ctx
                a = wcast_matmul(state[tag][1], w_down, li, f"mla_down_{tag}")
                k, vt = kv_up(a, gkv, li, w_ukv, cos, sin, f"kv_up_{tag}")
                need_q = tag == "lat" or not last
                qt = q_up(a, gq, li, w_up_q, cos, sin, f"q_up_{tag}") if need_q else None
                proj[tag] = (qt, k, vt)
            qt, k, vt = proj["lat"]
            qtc, kc, vtc = proj["ctx"]
            branch["lat"] = flash_attention(qt, k, vt, kc, vtc, "attn_lat")
            if not last:
                branch["ctx"] = flash_attention(qtc, kc, vtc, None, None, "attn_ctx")
            w_proj, proj_name = mla_w_o, "attn_out"
        else:
            for tag in tags:
                z = matmul_cols(state[tag][1], w_in, li, 512, f"even_in_{tag}")
                branch[tag] = even_mid(z, pw, ps, conv_w, li, f"even_mid_{tag}")
            w_proj, proj_name = even_w_out, "even_out"

        ffn_in = {}
        for tag, y in branch.items():
            x1, h2 = matmul_residual(y, w_proj, li, state[tag][0], mods, layer, rows[tag], g2,
                                     f"{proj_name}_{tag}")
            ffn_in[tag] = (h2, x1)
        state.update(ffn_residual(ffn_in["lat"], ffn_in.get("ctx"), mods, layer, ffn_w_gate, ffn_w_up, wd,
                                  final_g if last else g1, last, "ffn"))

    return state["lat"][0][None]
```

```python
import functools

import jax
import jax.numpy as jnp
import numpy as np
from jax import lax
from jax.experimental import pallas as pl
from jax.experimental.pallas import tpu as pltpu

D_MODEL = 2048
DEPTH = 4
GRID_W = 64
EPS = 1e-6
POOL_WINDOWS = (2, 4, 8, 16)
POOL_WIDTH = 1024
POOL_GROUP = 256
CONV_WIDTH = 1024
EVEN_IN = POOL_WIDTH + 3 * CONV_WIDTH
MLA_HEADS = 16
QK_NOPE = 128
QK_ROPE = 64
V_DIM = 128
Q_LORA = 512
KV_LORA = 512
ROPE_THETA = 10000.0
D_FF = 5632

V7X_LANES = 128
V7X_SUBLANES = 8
V7X_MXU_DIM = 256

HEAD_PAD = V7X_MXU_DIM
HALO = V7X_SUBLANES
LAT_ROW, CTX_ROW = 0, 1
BF16_SUBLANES = 2 * V7X_SUBLANES
VT_ROWS = V_DIM + BF16_SUBLANES
ATTN_TQ = 1024
ATTN_PIECE = 2 * V7X_MXU_DIM
ATTN_AHEAD = 2
HEAD_ROWS = 256
BF16 = jnp.bfloat16
F32 = jnp.float32


def _params(semantics, vmem_mb):
    return pltpu.CompilerParams(dimension_semantics=semantics,
                                vmem_limit_bytes=vmem_mb * 1024 * 1024)


def _tile_rows(m):
    return min(m, 1024)


def _dot(a, b):
    return jnp.dot(a, b, preferred_element_type=F32)


def _rms(x, g):
    return x * lax.rsqrt(jnp.mean(x * x, axis=-1, keepdims=True) + EPS) * g


def _normmod(x, g_ref, sh_ref, sc_ref, row):
    inv = lax.rsqrt(jnp.mean(x * x, axis=-1, keepdims=True) + EPS)
    gain = g_ref[0] * (1.0 + sc_ref[0, row:row + 1, :])
    return (x * inv * gain + sh_ref[0, row:row + 1, :]).astype(BF16)


def _mod_spec(layer, chunk, ngrid):
    if ngrid == 1:
        return pl.BlockSpec((1, V7X_SUBLANES, D_MODEL), lambda i: (layer, 0, chunk))
    return pl.BlockSpec((1, V7X_SUBLANES, D_MODEL), lambda i, j: (layer, 0, chunk))


def _gain_spec(layer, width, ngrid):
    if ngrid == 1:
        return pl.BlockSpec((1, 1, width), lambda i: (layer, 0, 0))
    return pl.BlockSpec((1, 1, width), lambda i, j: (layer, 0, 0))


def _ada_kernel(cond_ref, w_ref, b_ref, o_ref):
    c = cond_ref[...]
    s = (c * jax.nn.sigmoid(c)).astype(BF16)
    o_ref[0] = _dot(s, w_ref[0].astype(BF16)) + b_ref[0]


def ada_table(cond8, ada_w, ada_b):
    tn = 1024
    n_out = 6 * D_MODEL
    return pl.pallas_call(
        _ada_kernel,
        grid=(DEPTH, n_out // tn),
        in_specs=[pl.BlockSpec((V7X_SUBLANES, D_MODEL), lambda l, j: (0, 0)),
                  pl.BlockSpec((1, D_MODEL, tn), lambda l, j: (l, 0, j)),
                  pl.BlockSpec((1, 1, tn), lambda l, j: (l, 0, j))],
        out_specs=pl.BlockSpec((1, V7X_SUBLANES, tn), lambda l, j: (l, 0, j)),
        out_shape=jax.ShapeDtypeStruct((DEPTH, V7X_SUBLANES, n_out), F32),
        compiler_params=_params(("parallel", "parallel"), 40),
        name="ada_table",
    )(cond8, ada_w, ada_b.reshape(DEPTH, 1, n_out))


def _matmul_cols_kernel(h_ref, w_ref, o_ref):
    o_ref[...] = _dot(h_ref[...], w_ref[0])


def matmul_cols(h, w, li, tn, name):
    m, k = h.shape
    tm = _tile_rows(m)
    n_out = w.shape[2]
    return pl.pallas_call(
        _matmul_cols_kernel,
        grid=(m // tm, n_out // tn),
        in_specs=[pl.BlockSpec((tm, k), lambda i, j: (i, 0)),
                  pl.BlockSpec((1, k, tn), lambda i, j: (li, 0, j))],
        out_specs=pl.BlockSpec((tm, tn), lambda i, j: (i, j)),
        out_shape=jax.ShapeDtypeStruct((m, n_out), F32),
        compiler_params=_params(("parallel", "parallel"), 40),
        name=name,
    )(h, w)


def _normmod_kernel(row, x_ref, g_ref, sh_ref, sc_ref, h_ref):
    h_ref[...] = _normmod(x_ref[...], g_ref, sh_ref, sc_ref, row)


def normmod(x, gains, mods, layer, row, chunk, name):
    m = x.shape[0]
    tm = min(m, 512)
    return pl.pallas_call(
        functools.partial(_normmod_kernel, row),
        grid=(m // tm,),
        in_specs=[pl.BlockSpec((tm, D_MODEL), lambda i: (i, 0)),
                  _gain_spec(layer, D_MODEL, 1),
                  _mod_spec(layer, chunk, 1),
                  _mod_spec(layer, chunk + 1, 1)],
        out_specs=pl.BlockSpec((tm, D_MODEL), lambda i: (i, 0)),
        out_shape=jax.ShapeDtypeStruct((m, D_MODEL), BF16),
        compiler_params=_params(("parallel",), 32),
        name=name,
    )(x, gains, mods, mods)


def _round_weights(pairs):
    @pl.when(pl.program_id(1) == 0)
    def _():
        for w_ref, w_s in pairs:
            w_s[...] = w_ref[0].astype(BF16)


def _wcast_matmul_kernel(h_ref, w_ref, o_ref, w_s):
    _round_weights([(w_ref, w_s)])
    o_ref[...] = _dot(h_ref[...], w_s[...])


def wcast_matmul(h, w, li, name):
    m, k = h.shape
    n_out = w.shape[2]
    tm = _tile_rows(m)
    return pl.pallas_call(
        _wcast_matmul_kernel,
        grid=(1, m // tm),
        in_specs=[pl.BlockSpec((tm, k), lambda j, i: (i, 0)),
                  pl.BlockSpec((1, k, n_out), lambda j, i: (li, 0, 0), pipeline_mode=pl.Buffered(1))],
        out_specs=pl.BlockSpec((tm, n_out), lambda j, i: (i, 0)),
        out_shape=jax.ShapeDtypeStruct((m, n_out), F32),
        scratch_shapes=[pltpu.VMEM((k, n_out), BF16)],
        compiler_params=_params(("parallel", "arbitrary"), 48),
        name=name,
    )(h, w)


def _matmul_residual_kernel(n_lat, with_ctx, a_ref, w_ref, x_ref, gate_ref, g_ref, sh_ref, sc_ref, *rest):
    def rows(a, x, row, o_ref, h_ref):
        out = x[...] + gate_ref[0, row:row + 1, :] * _dot(a[...], w_s[...])
        o_ref[...] = out
        h_ref[...] = _normmod(out, g_ref, sh_ref, sc_ref, row)

    if with_ctx:
        ac_ref, xc_ref, o_ref, h_ref, oc_ref, hc_ref, w_s = rest
    else:
        o_ref, h_ref, w_s = rest
    _round_weights([(w_ref, w_s)])
    rows(a_ref, x_ref, LAT_ROW, o_ref, h_ref)
    if with_ctx:
        pl.when(pl.program_id(1) == n_lat - 1)(lambda: rows(ac_ref, xc_ref, CTX_ROW, oc_ref, hc_ref))


def matmul_residual(lat, ctx, w, li, mods, layer, gains2, name):
    a, x = lat
    m, k = a.shape
    tm = HEAD_ROWS
    with_ctx = ctx is not None
    row_spec = pl.BlockSpec((tm, D_MODEL), lambda j, i: (i, 0))
    in_specs = [pl.BlockSpec((tm, k), lambda j, i: (i, 0)),
                pl.BlockSpec((1, k, D_MODEL), lambda j, i: (li, 0, 0), pipeline_mode=pl.Buffered(1)),
                row_spec,
                _mod_spec(layer, 2, 2),
                _gain_spec(layer, D_MODEL, 2),
                _mod_spec(layer, 3, 2),
                _mod_spec(layer, 4, 2)]
    out_specs = [row_spec, row_spec]
    out_shape = [jax.ShapeDtypeStruct((m, D_MODEL), F32), jax.ShapeDtypeStruct((m, D_MODEL), BF16)]
    args = [a, w, x, mods, gains2, mods, mods]
    if with_ctx:
        ac, xc = ctx
        mc = xc.shape[0]
        whole = lambda cols: pl.BlockSpec((mc, cols), lambda j, i: (0, 0), pipeline_mode=pl.Buffered(1))
        in_specs += [whole(k), whole(D_MODEL)]
        args += [ac, xc]
        out_specs += [pl.BlockSpec((mc, D_MODEL), lambda j, i: (0, 0))] * 2
        out_shape += [jax.ShapeDtypeStruct((mc, D_MODEL), F32), jax.ShapeDtypeStruct((mc, D_MODEL), BF16)]
    outs = pl.pallas_call(
        functools.partial(_matmul_residual_kernel, m // tm, with_ctx),
        grid=(1, m // tm),
        in_specs=in_specs,
        out_specs=out_specs,
        out_shape=out_shape,
        scratch_shapes=[pltpu.VMEM((k, D_MODEL), BF16)],
        compiler_params=_params(("parallel", "arbitrary"), 56),
        name=name,
    )(*args)
    result = {"lat": (outs[1], outs[0])}
    if with_ctx:
        result["ctx"] = (outs[3], outs[2])
    return result


def _ffn_up_kernel(n_lat, with_ctx, *refs):
    if with_ctx:
        h_ref, wg_ref, wu_ref, hc_ref, a_ref, ac_ref, wg_s, wu_s = refs
    else:
        h_ref, wg_ref, wu_ref, a_ref, wg_s, wu_s = refs
    _round_weights([(wg_ref, wg_s), (wu_ref, wu_s)])

    def swiglu(src_ref, dst_ref):
        h = src_ref[...]
        g = _dot(h, wg_s[...])
        u = _dot(h, wu_s[...])
        dst_ref[...] = (g * jax.nn.sigmoid(g) * u).astype(dst_ref.dtype)

    swiglu(h_ref, a_ref)
    if with_ctx:
        pl.when(pl.program_id(1) == n_lat - 1)(lambda: swiglu(hc_ref, ac_ref))


def _ffn_down_kernel(n_lat, final_norm, with_ctx, a_ref, wd_ref, x_ref, gate_ref, g_ref, sh_ref, sc_ref, *rest):
    def rows(a, x, row, o_ref, h_ref):
        out = x[...] + gate_ref[0, row:row + 1, :] * _dot(a[...], wd_ref[0])
        if final_norm:
            o_ref[...] = _rms(out, g_ref[0])
        else:
            o_ref[...] = out
            h_ref[...] = _normmod(out, g_ref, sh_ref, sc_ref, row)

    if with_ctx:
        ac_ref, xc_ref, o_ref, h_ref, oc_ref, hc_ref = rest
        rows(a_ref, x_ref, LAT_ROW, o_ref, h_ref)
        pl.when(pl.program_id(0) == n_lat - 1)(lambda: rows(ac_ref, xc_ref, CTX_ROW, oc_ref, hc_ref))
    else:
        rows(a_ref, x_ref, LAT_ROW, rest[0], None if final_norm else rest[1])


def ffn_residual(lat, ctx, mods, layer, wg, wu, wd, next_gains, final_norm, name):
    h, x = lat
    m = x.shape[0]
    tm = _tile_rows(m)
    tf = 512
    with_ctx = ctx is not None
    n_up = m // tm
    in_specs = [pl.BlockSpec((tm, D_MODEL), lambda f, i: (i, 0)),
                pl.BlockSpec((1, D_MODEL, tf), lambda f, i: (layer, 0, f)),
                pl.BlockSpec((1, D_MODEL, tf), lambda f, i: (layer, 0, f))]
    out_specs = [pl.BlockSpec((tm, tf), lambda f, i: (i, f))]
    out_shape = [jax.ShapeDtypeStruct((m, D_FF), BF16)]
    args = [h, wg, wu]
    if with_ctx:
        hc, xc = ctx
        mc = xc.shape[0]
        in_specs.append(pl.BlockSpec((mc, D_MODEL), lambda f, i: (0, 0)))
        out_specs.append(pl.BlockSpec((mc, tf), lambda f, i: (0, f)))
        out_shape.append(jax.ShapeDtypeStruct((mc, D_FF), BF16))
        args.append(hc)
    acts = pl.pallas_call(
        functools.partial(_ffn_up_kernel, n_up, with_ctx),
        grid=(D_FF // tf, n_up),
        in_specs=in_specs,
        out_specs=out_specs,
        out_shape=out_shape,
        scratch_shapes=[pltpu.VMEM((D_MODEL, tf), BF16), pltpu.VMEM((D_MODEL, tf), BF16)],
        compiler_params=_params(("parallel", "arbitrary"), 48),
        name=name + "_up",
    )(*args)

    tr = HEAD_ROWS
    n_dn = m // tr
    nxt = 0 if final_norm else layer + 1
    row_spec = pl.BlockSpec((tr, D_MODEL), lambda i: (i, 0))
    in_specs = [pl.BlockSpec((tr, D_FF), lambda i: (i, 0)),
                pl.BlockSpec((1, D_FF, D_MODEL), lambda i: (layer, 0, 0), pipeline_mode=pl.Buffered(1)),
                row_spec,
                _mod_spec(layer, 5, 1),
                _gain_spec(nxt, D_MODEL, 1),
                _mod_spec(nxt, 0, 1),
                _mod_spec(nxt, 1, 1)]
    out_specs = [row_spec]
    out_shape = [jax.ShapeDtypeStruct((m, D_MODEL), F32)]
    args = [acts[0], wd, x, mods, next_gains, mods, mods]
    if not final_norm:
        out_specs.append(row_spec)
        out_shape.append(jax.ShapeDtypeStruct((m, D_MODEL), BF16))
    if with_ctx:
        whole = lambda cols: pl.BlockSpec((mc, cols), lambda i: (0, 0), pipeline_mode=pl.Buffered(1))
        in_specs += [whole(D_FF), whole(D_MODEL)]
        args += [acts[1], xc]
        out_specs += [pl.BlockSpec((mc, D_MODEL), lambda i: (0, 0))] * 2
        out_shape += [jax.ShapeDtypeStruct((mc, D_MODEL), F32), jax.ShapeDtypeStruct((mc, D_MODEL), BF16)]
    outs = pl.pallas_call(
        functools.partial(_ffn_down_kernel, n_dn, final_norm, with_ctx),
        grid=(n_dn,),
        in_specs=in_specs,
        out_specs=out_specs,
        out_shape=out_shape,
        compiler_params=_params(("arbitrary",), 56),
        name=name + "_down",
    )(*args)
    if final_norm:
        return {"lat": (outs[0], None)}
    state = {"lat": (outs[0], outs[1])}
    if with_ctx:
        state["ctx"] = (outs[2], outs[3])
    return state


def _shift_rows(x, k):
    n = x.shape[0]
    return pltpu.roll(x, (n - k) % n, axis=0)


def _even_mid_kernel(seq_len, zp_ref, z_ref, zn_ref, pw_ref, ps_ref, cw_ref, o_ref):
    i = pl.program_id(0)
    tm = z_ref.shape[0]
    first = i == 0
    last = i == pl.num_programs(0) - 1
    row = i * tm + lax.broadcasted_iota(jnp.int32, (tm, 1), 0)

    def with_halo(lo, hi):
        before = jnp.where(first, 0.0, zp_ref[:, lo:hi])
        after = jnp.where(last, 0.0, zn_ref[:, lo:hi])
        return jnp.concatenate([before, z_ref[:, lo:hi], after], axis=0)

    for g, w in enumerate(POOL_WINDOWS):
        lo_c, hi_c = g * POOL_GROUP, (g + 1) * POOL_GROUP
        u = with_halo(lo_c, hi_c)
        b = u
        s = 1
        while s < w:
            b = b + _shift_rows(b, s)
            s *= 2
        win = _shift_rows(b, -(w // 2))[HALO:HALO + tm]
        cnt = (jnp.minimum(row + (w - w // 2), seq_len) - jnp.maximum(row - w // 2, 0)).astype(F32)
        p = win / cnt - z_ref[:, lo_c:hi_c]
        y = _dot(p.astype(BF16), pw_ref[0, g]) * ps_ref[0, :, lo_c:hi_c]
        o_ref[:, lo_c:hi_c] = y.astype(o_ref.dtype)

    c0 = POOL_WIDTH
    gate_b = z_ref[:, c0:c0 + CONV_WIDTH]
    u = with_halo(c0 + CONV_WIDTH, c0 + 2 * CONV_WIDTH) * with_halo(c0 + 2 * CONV_WIDTH, c0 + 3 * CONV_WIDTH)
    conv = (_shift_rows(u, -1) * cw_ref[0, 0:1, :] + u * cw_ref[0, 1:2, :] + _shift_rows(u, 1) * cw_ref[0, 2:3, :])
    y_b = gate_b * conv[HALO:HALO + tm]
    o_ref[:, POOL_WIDTH:] = y_b.astype(o_ref.dtype)


def even_mid(z, pool_w, pool_scale, conv_w, li, name):
    m = z.shape[0]
    tm = min(m, 512)
    nb = tm // HALO
    last_blk = m // HALO - 1
    n_win = len(POOL_WINDOWS)
    return pl.pallas_call(
        functools.partial(_even_mid_kernel, m),
        grid=(m // tm,),
        in_specs=[pl.BlockSpec((HALO, EVEN_IN), lambda i: (jnp.maximum(i * nb - 1, 0), 0)),
                  pl.BlockSpec((tm, EVEN_IN), lambda i: (i, 0)),
                  pl.BlockSpec((HALO, EVEN_IN), lambda i: (jnp.minimum((i + 1) * nb, last_blk), 0)),
                  pl.BlockSpec((1, n_win, POOL_GROUP, POOL_GROUP), lambda i: (li, 0, 0, 0)),
                  pl.BlockSpec((1, 1, POOL_WIDTH), lambda i: (li, 0, 0)),
                  pl.BlockSpec((1, 3, CONV_WIDTH), lambda i: (li, 0, 0))],
        out_specs=pl.BlockSpec((tm, POOL_WIDTH + CONV_WIDTH), lambda i: (i, 0)),
        out_shape=jax.ShapeDtypeStruct((m, POOL_WIDTH + CONV_WIDTH), BF16),
        compiler_params=_params(("parallel",), 48),
        name=name,
    )(z, z, z, pool_w, pool_scale, conv_w)


def _rope_lanes(t, cos_ref, sin_ref):
    return t * cos_ref[...] + pltpu.roll(t, QK_ROPE, axis=1) * sin_ref[...]


def _q_up_kernel(scale, a_ref, g_ref, w_ref, cos_ref, sin_ref, qt_ref):
    n = _rms(a_ref[...], g_ref[0]).astype(BF16)
    q = _dot(n, w_ref[0])
    for h in range(MLA_HEADS):
        base = h * HEAD_PAD
        rope = _rope_lanes(q[:, base + QK_NOPE:base + HEAD_PAD], cos_ref, sin_ref)
        qt_ref[h, :QK_NOPE, :] = (q[:, base:base + QK_NOPE] * scale).T.astype(qt_ref.dtype)
        qt_ref[h, QK_NOPE:, :] = (rope * scale).T.astype(qt_ref.dtype)


def q_up(qkv_a, gains, li, w_uq, cos, sin, name):
    m = qkv_a.shape[0]
    tr = HEAD_ROWS
    scale = float((QK_NOPE + QK_ROPE) ** -0.5 * np.log2(np.e))
    return pl.pallas_call(
        functools.partial(_q_up_kernel, scale),
        grid=(m // tr,),
        in_specs=[pl.BlockSpec((tr, Q_LORA), lambda i: (i, 0)),
                  _gain_spec(li, Q_LORA, 1),
                  pl.BlockSpec((1, Q_LORA, MLA_HEADS * HEAD_PAD), lambda i: (li, 0, 0)),
                  pl.BlockSpec((tr, V7X_LANES), lambda i: (i, 0)),
                  pl.BlockSpec((tr, V7X_LANES), lambda i: (i, 0))],
        out_specs=pl.BlockSpec((MLA_HEADS, HEAD_PAD, tr), lambda i: (0, 0, i)),
        out_shape=jax.ShapeDtypeStruct((MLA_HEADS, HEAD_PAD, m), BF16),
        compiler_params=_params(("parallel",), 40),
        name=name,
    )(qkv_a, gains, w_uq, cos, sin)


def _kv_up_kernel(c_ref, r_ref, g_ref, w_ref, cos_ref, sin_ref, k_ref, vt_ref):
    n = _rms(c_ref[...], g_ref[0]).astype(BF16)
    k_rope = _rope_lanes(r_ref[...], cos_ref, sin_ref).astype(k_ref.dtype)
    kv = _dot(n, w_ref[0])
    ones = jnp.ones((VT_ROWS - V_DIM, vt_ref.shape[2]), vt_ref.dtype)
    width = QK_NOPE + V_DIM
    for h in range(MLA_HEADS):
        base = h * width
        k_ref[h, :, :QK_NOPE] = kv[:, base:base + QK_NOPE].astype(k_ref.dtype)
        k_ref[h, :, QK_NOPE:] = k_rope
        vt_ref[h, :V_DIM, :] = kv[:, base + QK_NOPE:base + width].T.astype(vt_ref.dtype)
        vt_ref[h, V_DIM:, :] = ones


def kv_up(qkv_a, gains, li, w_ukv, cos, sin, name):
    m = qkv_a.shape[0]
    tr = HEAD_ROWS
    rope_blk = (Q_LORA + KV_LORA) // V7X_LANES
    return pl.pallas_call(
        _kv_up_kernel,
        grid=(m // tr,),
        in_specs=[pl.BlockSpec((tr, KV_LORA), lambda i: (i, 1)),
                  pl.BlockSpec((tr, V7X_LANES), lambda i: (i, rope_blk)),
                  _gain_spec(li, KV_LORA, 1),
                  pl.BlockSpec((1, KV_LORA, MLA_HEADS * (QK_NOPE + V_DIM)), lambda i: (li, 0, 0)),
                  pl.BlockSpec((tr, V7X_LANES), lambda i: (i, 0)),
                  pl.BlockSpec((tr, V7X_LANES), lambda i: (i, 0))],
        out_specs=[pl.BlockSpec((MLA_HEADS, tr, HEAD_PAD), lambda i: (0, i, 0)),
                   pl.BlockSpec((MLA_HEADS, VT_ROWS, tr), lambda i: (0, 0, i))],
        out_shape=[jax.ShapeDtypeStruct((MLA_HEADS, m, HEAD_PAD), BF16),
                   jax.ShapeDtypeStruct((MLA_HEADS, VT_ROWS, m), BF16)],
        compiler_params=_params(("parallel",), 40),
        name=name,
    )(qkv_a, qkv_a, gains, w_ukv, cos, sin)


def _flash_kernel(piece, n_lat, has_ctx, *refs):
    if has_ctx:
        qt_ref, k_ref, vt_ref, kc_ref, vtc_ref, o_ref = refs
    else:
        qt_ref, k_ref, vt_ref, o_ref = refs
    qt = qt_ref[0]
    tq = qt.shape[1]
    n_all = n_lat + int(has_ctx)

    def keys(i):
        return kc_ref[0] if i >= n_lat else k_ref[0, i * piece:(i + 1) * piece, :]

    def values_t(i):
        return vtc_ref[0] if i >= n_lat else vt_ref[0, :, i * piece:(i + 1) * piece]

    m = jnp.full((1, tq), -jnp.inf, F32)
    acc = jnp.zeros((VT_ROWS, tq), F32)
    pending = [_dot(keys(i), qt) for i in range(min(ATTN_AHEAD, n_all))]
    for t in range(n_all):
        s = pending.pop(0)
        if t + ATTN_AHEAD < n_all:
            pending.append(_dot(keys(t + ATTN_AHEAD), qt))
        m_new = jnp.maximum(m, jnp.max(s, axis=0, keepdims=True))
        p = jnp.exp2(s - m_new).astype(BF16)
        acc = jnp.exp2(m - m_new) * acc + _dot(values_t(t), p)
        m = m_new
    o_ref[...] = (acc[:V_DIM] / acc[V_DIM:V_DIM + 1]).T.astype(o_ref.dtype)


def flash_attention(qt, k, vt, k_ctx, vt_ctx, name):
    h, _, m = qt.shape
    n_k = k.shape[1]
    tq = min(m, ATTN_TQ)
    piece = min(n_k, ATTN_PIECE)
    has_ctx = k_ctx is not None
    in_specs = [pl.BlockSpec((1, HEAD_PAD, tq), lambda hh, i: (hh, 0, i)),
                pl.BlockSpec((1, n_k, HEAD_PAD), lambda hh, i: (hh, 0, 0)),
                pl.BlockSpec((1, VT_ROWS, n_k), lambda hh, i: (hh, 0, 0))]
    args = [qt, k, vt]
    if has_ctx:
        n_c = k_ctx.shape[1]
        in_specs += [pl.BlockSpec((1, n_c, HEAD_PAD), lambda hh, i: (hh, 0, 0)),
                     pl.BlockSpec((1, VT_ROWS, n_c), lambda hh, i: (hh, 0, 0))]
        args += [k_ctx, vt_ctx]
    return pl.pallas_call(
        functools.partial(_flash_kernel, piece, n_k // piece, has_ctx),
        grid=(h, m // tq),
        in_specs=in_specs,
        out_specs=pl.BlockSpec((tq, V_DIM), lambda hh, i: (i, hh)),
        out_shape=jax.ShapeDtypeStruct((m, h * V_DIM), BF16),
        compiler_params=_params(("parallel", "parallel"), 48),
        name=name,
    )(*args)


def _swap_pairs(w):
    q = QK_ROPE // 4
    r1, r2, c1, c2 = (w[..., j * q:(j + 1) * q] for j in range(4))
    return jnp.concatenate([-r2, r1, -c2, c1], axis=-1)


def _rope_tables(n):
    q = QK_ROPE // 4
    rows = n // GRID_W
    inv = jnp.power(jnp.float32(ROPE_THETA), -jnp.arange(q, dtype=F32) / q)
    ar = jnp.arange(rows).astype(F32)[:, None] * inv
    ac = jnp.arange(GRID_W).astype(F32)[:, None] * inv
    cos_r, sin_r = (jnp.repeat(f(ar), GRID_W, axis=0) for f in (jnp.cos, jnp.sin))
    cos_c, sin_c = (jnp.tile(f(ac), (rows, 1)) for f in (jnp.cos, jnp.sin))
    zeros = jnp.zeros((n, V7X_LANES - QK_ROPE), F32)
    cos = jnp.concatenate([cos_r, cos_r, cos_c, cos_c, zeros], axis=1)
    sin = jnp.concatenate([sin_r, sin_r, sin_c, sin_c, zeros], axis=1)
    return cos, sin


def _identity_tables(n):
    cos = jnp.concatenate([jnp.ones((n, QK_ROPE), F32), jnp.zeros((n, V7X_LANES - QK_ROPE), F32)], axis=1)
    return cos, jnp.zeros((n, V7X_LANES), F32)


def _mla_weights(w_dq, w_uq, w_dkv):
    n = w_dq.shape[0]
    rope = w_dkv[..., KV_LORA:]
    w_down = jnp.concatenate([w_dq, w_dkv[..., :KV_LORA], rope, _swap_pairs(rope)], axis=-1)
    wq = w_uq.reshape(n, Q_LORA, MLA_HEADS, QK_NOPE + QK_ROPE)
    w_up_q = jnp.concatenate([wq, _swap_pairs(wq[..., QK_NOPE:])], axis=-1)
    return w_down, w_up_q.reshape(n, Q_LORA, MLA_HEADS * HEAD_PAD).astype(BF16)


def kernel(x, c, ctx, c_ctx, ada_w, ada_b, norm1_g, norm2_g, even_w_in, pool_w, pool_scale, conv_w,
           even_w_out, mla_w_dq, mla_q_norm_g, mla_w_uq, mla_w_dkv, mla_kv_norm_g, mla_w_ukv, mla_w_o,
           ffn_w_gate, ffn_w_up, ffn_w_down, final_norm_g):
    n_lat, n_ctx = x.shape[1], ctx.shape[1]
    x_lat, x_ctx = x[0], ctx[0]

    cond8 = jnp.concatenate([c, c_ctx[None, :], jnp.zeros((V7X_SUBLANES - 2, D_MODEL), F32)], axis=0)
    mods = ada_table(cond8, ada_w, ada_b)
    g1 = norm1_g.reshape(DEPTH, 1, D_MODEL)
    g2 = norm2_g.reshape(DEPTH, 1, D_MODEL)
    gq = mla_q_norm_g.reshape(-1, 1, Q_LORA)
    gkv = mla_kv_norm_g.reshape(-1, 1, KV_LORA)
    final_g = final_norm_g.reshape(1, 1, D_MODEL)
    rope_lat = _rope_tables(n_lat)
    rope_ctx = _identity_tables(n_ctx)

    w_in, pw = even_w_in.astype(BF16), pool_w.astype(BF16)
    ps = pool_scale.reshape(-1, 1, POOL_WIDTH)
    w_down, w_up_q = _mla_weights(mla_w_dq, mla_w_uq, mla_w_dkv)
    w_ukv = mla_w_ukv.astype(BF16)
    wd = ffn_w_down.astype(BF16)

    state = {"lat": (x_lat, normmod(x_lat, g1, mods, 0, LAT_ROW, 0, "norm_in_lat")),
             "ctx": (x_ctx, normmod(x_ctx, g1, mods, 0, CTX_ROW, 0, "norm_in_ctx"))}

    for layer in range(DEPTH):
        last = layer == DEPTH - 1
        odd = layer % 2 == 1
        li = layer // 2
        tags = ["lat"] + (["ctx"] if odd or not last else [])
        branch = {}

        if odd:
            proj = {}
            for tag in tags:
                cos, sin = rope_lat if tag == "lat" else rope_ctx
                a = wcast_matmul(state[tag][1], w_down, li, f"mla_down_{tag}")
                k, vt = kv_up(a, gkv, li, w_ukv, cos, sin, f"kv_up_{tag}")
                need_q = tag == "lat" or not last
                qt = q_up(a, gq, li, w_up_q, cos, sin, f"q_up_{tag}") if need_q else None
                proj[tag] = (qt, k, vt)
            qt, k, vt = proj["lat"]
            qtc, kc, vtc = proj["ctx"]
            branch["lat"] = flash_attention(qt, k, vt, kc, vtc, "attn_lat")
            if not last:
                branch["ctx"] = flash_attention(qtc, kc, vtc, None, None, "attn_ctx")
            w_proj, proj_name = mla_w_o, "attn_out"
        else:
            for tag in tags:
                z = matmul_cols(state[tag][1], w_in, li, 512, f"even_in_{tag}")
                branch[tag] = even_mid(z, pw, ps, conv_w, li, f"even_mid_{tag}")
            w_proj, proj_name = even_w_out, "even_out"

        proj_in = {tag: (y, state[tag][0]) for tag, y in branch.items()}
        ffn_in = matmul_residual(proj_in["lat"], proj_in.get("ctx"), w_proj, li, mods, layer, g2, proj_name)
        state.update(ffn_residual(ffn_in["lat"], ffn_in.get("ctx"), mods, layer, ffn_w_gate, ffn_w_up, wd,
                                  final_g if last else g1, last, "ffn"))

    return state["lat"][0][None]
```

```python
import functools

import jax
import jax.numpy as jnp
import numpy as np
from jax import lax
from jax.experimental import pallas as pl
from jax.experimental.pallas import tpu as pltpu

D_MODEL = 2048
DEPTH = 4
GRID_W = 64
EPS = 1e-6
POOL_WINDOWS = (2, 4, 8, 16)
POOL_WIDTH = 1024
POOL_GROUP = 256
CONV_WIDTH = 1024
EVEN_IN = POOL_WIDTH + 3 * CONV_WIDTH
MLA_HEADS = 16
QK_NOPE = 128
QK_ROPE = 64
V_DIM = 128
Q_LORA = 512
KV_LORA = 512
ROPE_THETA = 10000.0
D_FF = 5632

V7X_LANES = 128
V7X_SUBLANES = 8
V7X_MXU_DIM = 256

HEAD_PAD = V7X_MXU_DIM
HALO = V7X_SUBLANES
LAT_ROW, CTX_ROW = 0, 1
BF16_SUBLANES = 2 * V7X_SUBLANES
VT_ROWS = V_DIM + BF16_SUBLANES
ATTN_TQ = 1024
ATTN_PIECE = 2 * V7X_MXU_DIM
ATTN_AHEAD = 2
HEAD_ROWS = 256
BF16 = jnp.bfloat16
F32 = jnp.float32


def _params(semantics, vmem_mb):
    return pltpu.CompilerParams(dimension_semantics=semantics,
                                vmem_limit_bytes=vmem_mb * 1024 * 1024)


def _tile_rows(m):
    return min(m, 1024)


def _dot(a, b):
    return jnp.dot(a, b, preferred_element_type=F32)


def _rms(x, g):
    return x * lax.rsqrt(jnp.mean(x * x, axis=-1, keepdims=True) + EPS) * g


def _normmod(x, g_ref, sh_ref, sc_ref, row):
    inv = lax.rsqrt(jnp.mean(x * x, axis=-1, keepdims=True) + EPS)
    gain = g_ref[0] * (1.0 + sc_ref[0, row:row + 1, :])
    return (x * inv * gain + sh_ref[0, row:row + 1, :]).astype(BF16)


def _mod_spec(layer, chunk, ngrid):
    if ngrid == 1:
        return pl.BlockSpec((1, V7X_SUBLANES, D_MODEL), lambda i: (layer, 0, chunk))
    return pl.BlockSpec((1, V7X_SUBLANES, D_MODEL), lambda i, j: (layer, 0, chunk))


def _gain_spec(layer, width, ngrid):
    if ngrid == 1:
        return pl.BlockSpec((1, 1, width), lambda i: (layer, 0, 0))
    return pl.BlockSpec((1, 1, width), lambda i, j: (layer, 0, 0))


def _ada_kernel(cond_ref, w_ref, b_ref, o_ref):
    c = cond_ref[...]
    s = (c * jax.nn.sigmoid(c)).astype(BF16)
    o_ref[0] = _dot(s, w_ref[0].astype(BF16)) + b_ref[0]


def ada_table(cond8, ada_w, ada_b):
    tn = 1024
    n_out = 6 * D_MODEL
    return pl.pallas_call(
        _ada_kernel,
        grid=(DEPTH, n_out // tn),
        in_specs=[pl.BlockSpec((V7X_SUBLANES, D_MODEL), lambda l, j: (0, 0)),
                  pl.BlockSpec((1, D_MODEL, tn), lambda l, j: (l, 0, j)),
                  pl.BlockSpec((1, 1, tn), lambda l, j: (l, 0, j))],
        out_specs=pl.BlockSpec((1, V7X_SUBLANES, tn), lambda l, j: (l, 0, j)),
        out_shape=jax.ShapeDtypeStruct((DEPTH, V7X_SUBLANES, n_out), F32),
        compiler_params=_params(("parallel", "parallel"), 40),
        name="ada_table",
    )(cond8, ada_w, ada_b.reshape(DEPTH, 1, n_out))


def _matmul_cols_kernel(h_ref, w_ref, o_ref):
    o_ref[...] = _dot(h_ref[...], w_ref[0])


def matmul_cols(h, w, li, tn, name):
    m, k = h.shape
    tm = _tile_rows(m)
    n_out = w.shape[2]
    return pl.pallas_call(
        _matmul_cols_kernel,
        grid=(m // tm, n_out // tn),
        in_specs=[pl.BlockSpec((tm, k), lambda i, j: (i, 0)),
                  pl.BlockSpec((1, k, tn), lambda i, j: (li, 0, j))],
        out_specs=pl.BlockSpec((tm, tn), lambda i, j: (i, j)),
        out_shape=jax.ShapeDtypeStruct((m, n_out), F32),
        compiler_params=_params(("parallel", "parallel"), 40),
        name=name,
    )(h, w)


def _normmod_kernel(row, x_ref, g_ref, sh_ref, sc_ref, h_ref):
    h_ref[...] = _normmod(x_ref[...], g_ref, sh_ref, sc_ref, row)


def normmod(x, gains, mods, layer, row, chunk, name):
    m = x.shape[0]
    tm = min(m, 512)
    return pl.pallas_call(
        functools.partial(_normmod_kernel, row),
        grid=(m // tm,),
        in_specs=[pl.BlockSpec((tm, D_MODEL), lambda i: (i, 0)),
                  _gain_spec(layer, D_MODEL, 1),
                  _mod_spec(layer, chunk, 1),
                  _mod_spec(layer, chunk + 1, 1)],
        out_specs=pl.BlockSpec((tm, D_MODEL), lambda i: (i, 0)),
        out_shape=jax.ShapeDtypeStruct((m, D_MODEL), BF16),
        compiler_params=_params(("parallel",), 32),
        name=name,
    )(x, gains, mods, mods)


def _round_weights(pairs):
    @pl.when(pl.program_id(1) == 0)
    def _():
        for w_ref, w_s in pairs:
            w_s[...] = w_ref[0].astype(BF16)


def _wcast_matmul_kernel(h_ref, w_ref, o_ref, w_s):
    _round_weights([(w_ref, w_s)])
    o_ref[...] = _dot(h_ref[...], w_s[...])


def wcast_matmul(h, w, li, name):
    m, k = h.shape
    n_out = w.shape[2]
    tm = _tile_rows(m)
    return pl.pallas_call(
        _wcast_matmul_kernel,
        grid=(1, m // tm),
        in_specs=[pl.BlockSpec((tm, k), lambda j, i: (i, 0)),
                  pl.BlockSpec((1, k, n_out), lambda j, i: (li, 0, 0), pipeline_mode=pl.Buffered(1))],
        out_specs=pl.BlockSpec((tm, n_out), lambda j, i: (i, 0)),
        out_shape=jax.ShapeDtypeStruct((m, n_out), F32),
        scratch_shapes=[pltpu.VMEM((k, n_out), BF16)],
        compiler_params=_params(("parallel", "arbitrary"), 48),
        name=name,
    )(h, w)


def _matmul_residual_kernel(n_lat, with_ctx, a_ref, w_ref, x_ref, gate_ref, g_ref, sh_ref, sc_ref, *rest):
    def rows(a, x, row, o_ref, h_ref):
        out = x[...] + gate_ref[0, row:row + 1, :] * _dot(a[...], w_s[...])
        o_ref[...] = out
        h_ref[...] = _normmod(out, g_ref, sh_ref, sc_ref, row)

    if with_ctx:
        ac_ref, xc_ref, o_ref, h_ref, oc_ref, hc_ref, w_s = rest
    else:
        o_ref, h_ref, w_s = rest
    _round_weights([(w_ref, w_s)])
    rows(a_ref, x_ref, LAT_ROW, o_ref, h_ref)
    if with_ctx:
        pl.when(pl.program_id(1) == n_lat - 1)(lambda: rows(ac_ref, xc_ref, CTX_ROW, oc_ref, hc_ref))


def matmul_residual(lat, ctx, w, li, mods, layer, gains2, name):
    a, x = lat
    m, k = a.shape
    tm = HEAD_ROWS
    with_ctx = ctx is not None
    row_spec = pl.BlockSpec((tm, D_MODEL), lambda j, i: (i, 0))
    in_specs = [pl.BlockSpec((tm, k), lambda j, i: (i, 0)),
                pl.BlockSpec((1, k, D_MODEL), lambda j, i: (li, 0, 0), pipeline_mode=pl.Buffered(1)),
                row_spec,
                _mod_spec(layer, 2, 2),
                _gain_spec(layer, D_MODEL, 2),
                _mod_spec(layer, 3, 2),
                _mod_spec(layer, 4, 2)]
    out_specs = [row_spec, row_spec]
    out_shape = [jax.ShapeDtypeStruct((m, D_MODEL), F32), jax.ShapeDtypeStruct((m, D_MODEL), BF16)]
    args = [a, w, x, mods, gains2, mods, mods]
    if with_ctx:
        ac, xc = ctx
        mc = xc.shape[0]
        whole = lambda cols: pl.BlockSpec((mc, cols), lambda j, i: (0, 0), pipeline_mode=pl.Buffered(1))
        in_specs += [whole(k), whole(D_MODEL)]
        args += [ac, xc]
        out_specs += [pl.BlockSpec((mc, D_MODEL), lambda j, i: (0, 0))] * 2
        out_shape += [jax.ShapeDtypeStruct((mc, D_MODEL), F32), jax.ShapeDtypeStruct((mc, D_MODEL), BF16)]
    outs = pl.pallas_call(
        functools.partial(_matmul_residual_kernel, m // tm, with_ctx),
        grid=(1, m // tm),
        in_specs=in_specs,
        out_specs=out_specs,
        out_shape=out_shape,
        scratch_shapes=[pltpu.VMEM((k, D_MODEL), BF16)],
        compiler_params=_params(("parallel", "arbitrary"), 56),
        name=name,
    )(*args)
    result = {"lat": (outs[1], outs[0])}
    if with_ctx:
        result["ctx"] = (outs[3], outs[2])
    return result


def _ffn_up_kernel(n_lat, with_ctx, *refs):
    if with_ctx:
        h_ref, wg_ref, wu_ref, wd_ref, hc_ref, a_ref, wdb_ref, ac_ref, wg_s, wu_s = refs
    else:
        h_ref, wg_ref, wu_ref, wd_ref, a_ref, wdb_ref, wg_s, wu_s = refs
    _round_weights([(wg_ref, wg_s), (wu_ref, wu_s), (wd_ref, wdb_ref)])

    def swiglu(src_ref, dst_ref):
        h = src_ref[...]
        g = _dot(h, wg_s[...])
        u = _dot(h, wu_s[...])
        dst_ref[...] = (g * jax.nn.sigmoid(g) * u).astype(dst_ref.dtype)

    swiglu(h_ref, a_ref)
    if with_ctx:
        pl.when(pl.program_id(1) == n_lat - 1)(lambda: swiglu(hc_ref, ac_ref))


def _ffn_down_kernel(n_lat, final_norm, with_ctx, a_ref, wd_ref, x_ref, gate_ref, g_ref, sh_ref, sc_ref, *rest):
    def rows(a, x, row, o_ref, h_ref):
        out = x[...] + gate_ref[0, row:row + 1, :] * _dot(a[...], wd_ref[...])
        if final_norm:
            o_ref[...] = _rms(out, g_ref[0])
        else:
            o_ref[...] = out
            h_ref[...] = _normmod(out, g_ref, sh_ref, sc_ref, row)

    if with_ctx:
        ac_ref, xc_ref, o_ref, h_ref, oc_ref, hc_ref = rest
        rows(a_ref, x_ref, LAT_ROW, o_ref, h_ref)
        pl.when(pl.program_id(0) == n_lat - 1)(lambda: rows(ac_ref, xc_ref, CTX_ROW, oc_ref, hc_ref))
    else:
        rows(a_ref, x_ref, LAT_ROW, rest[0], None if final_norm else rest[1])


def ffn_residual(lat, ctx, mods, layer, wg, wu, wd, next_gains, final_norm, name):
    h, x = lat
    m = x.shape[0]
    tm = _tile_rows(m)
    tf = 512
    with_ctx = ctx is not None
    n_up = m // tm
    in_specs = [pl.BlockSpec((tm, D_MODEL), lambda f, i: (i, 0)),
                pl.BlockSpec((1, D_MODEL, tf), lambda f, i: (layer, 0, f)),
                pl.BlockSpec((1, D_MODEL, tf), lambda f, i: (layer, 0, f)),
                pl.BlockSpec((1, tf, D_MODEL), lambda f, i: (layer, f, 0))]
    out_specs = [pl.BlockSpec((tm, tf), lambda f, i: (i, f)),
                 pl.BlockSpec((tf, D_MODEL), lambda f, i: (f, 0))]
    out_shape = [jax.ShapeDtypeStruct((m, D_FF), BF16), jax.ShapeDtypeStruct((D_FF, D_MODEL), BF16)]
    args = [h, wg, wu, wd]
    if with_ctx:
        hc, xc = ctx
        mc = xc.shape[0]
        in_specs.append(pl.BlockSpec((mc, D_MODEL), lambda f, i: (0, 0)))
        out_specs.append(pl.BlockSpec((mc, tf), lambda f, i: (0, f)))
        out_shape.append(jax.ShapeDtypeStruct((mc, D_FF), BF16))
        args.append(hc)
    acts = pl.pallas_call(
        functools.partial(_ffn_up_kernel, n_up, with_ctx),
        grid=(D_FF // tf, n_up),
        in_specs=in_specs,
        out_specs=out_specs,
        out_shape=out_shape,
        scratch_shapes=[pltpu.VMEM((D_MODEL, tf), BF16), pltpu.VMEM((D_MODEL, tf), BF16)],
        compiler_params=_params(("parallel", "arbitrary"), 56),
        name=name + "_up",
    )(*args)

    tr = HEAD_ROWS
    n_dn = m // tr
    nxt = 0 if final_norm else layer + 1
    row_spec = pl.BlockSpec((tr, D_MODEL), lambda i: (i, 0))
    in_specs = [pl.BlockSpec((tr, D_FF), lambda i: (i, 0)),
                pl.BlockSpec((D_FF, D_MODEL), lambda i: (0, 0), pipeline_mode=pl.Buffered(1)),
                row_spec,
                _mod_spec(layer, 5, 1),
                _gain_spec(nxt, D_MODEL, 1),
                _mod_spec(nxt, 0, 1),
                _mod_spec(nxt, 1, 1)]
    out_specs = [row_spec]
    out_shape = [jax.ShapeDtypeStruct((m, D_MODEL), F32)]
    args = [acts[0], acts[1], x, mods, next_gains, mods, mods]
    if not final_norm:
        out_specs.append(row_spec)
        out_shape.append(jax.ShapeDtypeStruct((m, D_MODEL), BF16))
    if with_ctx:
        whole = lambda cols: pl.BlockSpec((mc, cols), lambda i: (0, 0), pipeline_mode=pl.Buffered(1))
        in_specs += [whole(D_FF), whole(D_MODEL)]
        args += [acts[2], xc]
        out_specs += [pl.BlockSpec((mc, D_MODEL), lambda i: (0, 0))] * 2
        out_shape += [jax.ShapeDtypeStruct((mc, D_MODEL), F32), jax.ShapeDtypeStruct((mc, D_MODEL), BF16)]
    outs = pl.pallas_call(
        functools.partial(_ffn_down_kernel, n_dn, final_norm, with_ctx),
        grid=(n_dn,),
        in_specs=in_specs,
        out_specs=out_specs,
        out_shape=out_shape,
        compiler_params=_params(("arbitrary",), 56),
        name=name + "_down",
    )(*args)
    if final_norm:
        return {"lat": (outs[0], None)}
    state = {"lat": (outs[0], outs[1])}
    if with_ctx:
        state["ctx"] = (outs[2], outs[3])
    return state


def _shift_rows(x, k):
    n = x.shape[0]
    return pltpu.roll(x, (n - k) % n, axis=0)


def _even_mid_kernel(seq_len, zp_ref, z_ref, zn_ref, pw_ref, ps_ref, cw_ref, o_ref):
    i = pl.program_id(0)
    tm = z_ref.shape[0]
    first = i == 0
    last = i == pl.num_programs(0) - 1
    row = i * tm + lax.broadcasted_iota(jnp.int32, (tm, 1), 0)

    def with_halo(lo, hi):
        before = jnp.where(first, 0.0, zp_ref[:, lo:hi])
        after = jnp.where(last, 0.0, zn_ref[:, lo:hi])
        return jnp.concatenate([before, z_ref[:, lo:hi], after], axis=0)

    for g, w in enumerate(POOL_WINDOWS):
        lo_c, hi_c = g * POOL_GROUP, (g + 1) * POOL_GROUP
        u = with_halo(lo_c, hi_c)
        b = u
        s = 1
        while s < w:
            b = b + _shift_rows(b, s)
            s *= 2
        win = _shift_rows(b, -(w // 2))[HALO:HALO + tm]
        cnt = (jnp.minimum(row + (w - w // 2), seq_len) - jnp.maximum(row - w // 2, 0)).astype(F32)
        p = win / cnt - z_ref[:, lo_c:hi_c]
        y = _dot(p.astype(BF16), pw_ref[0, g]) * ps_ref[0, :, lo_c:hi_c]
        o_ref[:, lo_c:hi_c] = y.astype(o_ref.dtype)

    c0 = POOL_WIDTH
    gate_b = z_ref[:, c0:c0 + CONV_WIDTH]
    u = with_halo(c0 + CONV_WIDTH, c0 + 2 * CONV_WIDTH) * with_halo(c0 + 2 * CONV_WIDTH, c0 + 3 * CONV_WIDTH)
    conv = (_shift_rows(u, -1) * cw_ref[0, 0:1, :] + u * cw_ref[0, 1:2, :] + _shift_rows(u, 1) * cw_ref[0, 2:3, :])
    y_b = gate_b * conv[HALO:HALO + tm]
    o_ref[:, POOL_WIDTH:] = y_b.astype(o_ref.dtype)


def even_mid(z, pool_w, pool_scale, conv_w, li, name):
    m = z.shape[0]
    tm = min(m, 512)
    nb = tm // HALO
    last_blk = m // HALO - 1
    n_win = len(POOL_WINDOWS)
    return pl.pallas_call(
        functools.partial(_even_mid_kernel, m),
        grid=(m // tm,),
        in_specs=[pl.BlockSpec((HALO, EVEN_IN), lambda i: (jnp.maximum(i * nb - 1, 0), 0)),
                  pl.BlockSpec((tm, EVEN_IN), lambda i: (i, 0)),
                  pl.BlockSpec((HALO, EVEN_IN), lambda i: (jnp.minimum((i + 1) * nb, last_blk), 0)),
                  pl.BlockSpec((1, n_win, POOL_GROUP, POOL_GROUP), lambda i: (li, 0, 0, 0)),
                  pl.BlockSpec((1, 1, POOL_WIDTH), lambda i: (li, 0, 0)),
                  pl.BlockSpec((1, 3, CONV_WIDTH), lambda i: (li, 0, 0))],
        out_specs=pl.BlockSpec((tm, POOL_WIDTH + CONV_WIDTH), lambda i: (i, 0)),
        out_shape=jax.ShapeDtypeStruct((m, POOL_WIDTH + CONV_WIDTH), BF16),
        compiler_params=_params(("parallel",), 48),
        name=name,
    )(z, z, z, pool_w, pool_scale, conv_w)


def _rope_lanes(t, cos_ref, sin_ref):
    return t * cos_ref[...] + pltpu.roll(t, QK_ROPE, axis=1) * sin_ref[...]


def _q_up_kernel(scale, a_ref, g_ref, wt_ref, cos_t_ref, sin_t_ref, qt_ref):
    n_t = _rms(a_ref[...], g_ref[0]).T.astype(BF16)
    cos_t, sin_t = cos_t_ref[...] * scale, sin_t_ref[...] * scale
    rope_end = QK_NOPE + QK_ROPE
    zeros = jnp.zeros((HEAD_PAD - rope_end, n_t.shape[1]), qt_ref.dtype)
    for h in range(MLA_HEADS):
        q_t = _dot(wt_ref[0, h * HEAD_PAD:(h + 1) * HEAD_PAD, :], n_t)
        qt_ref[h, :QK_NOPE, :] = (q_t[:QK_NOPE] * scale).astype(qt_ref.dtype)
        rope = q_t[QK_NOPE:rope_end] * cos_t + q_t[rope_end:] * sin_t
        qt_ref[h, QK_NOPE:rope_end, :] = rope.astype(qt_ref.dtype)
        qt_ref[h, rope_end:, :] = zeros


def q_up(qkv_a, gains, li, w_uq_t, cos_t, sin_t, name):
    m = qkv_a.shape[0]
    tr = HEAD_ROWS
    scale = float((QK_NOPE + QK_ROPE) ** -0.5 * np.log2(np.e))
    return pl.pallas_call(
        functools.partial(_q_up_kernel, scale),
        grid=(m // tr,),
        in_specs=[pl.BlockSpec((tr, Q_LORA), lambda i: (i, 0)),
                  _gain_spec(li, Q_LORA, 1),
                  pl.BlockSpec((1, MLA_HEADS * HEAD_PAD, Q_LORA), lambda i: (li, 0, 0)),
                  pl.BlockSpec((QK_ROPE, tr), lambda i: (0, i)),
                  pl.BlockSpec((QK_ROPE, tr), lambda i: (0, i))],
        out_specs=pl.BlockSpec((MLA_HEADS, HEAD_PAD, tr), lambda i: (0, 0, i)),
        out_shape=jax.ShapeDtypeStruct((MLA_HEADS, HEAD_PAD, m), BF16),
        compiler_params=_params(("parallel",), 40),
        name=name,
    )(qkv_a, gains, w_uq_t, cos_t, sin_t)


def _kv_up_kernel(c_ref, r_ref, g_ref, w_ref, cos_ref, sin_ref, k_ref, vt_ref):
    n = _rms(c_ref[...], g_ref[0]).astype(BF16)
    k_rope = _rope_lanes(r_ref[...], cos_ref, sin_ref).astype(k_ref.dtype)
    kv = _dot(n, w_ref[0])
    ones = jnp.ones((VT_ROWS - V_DIM, vt_ref.shape[2]), vt_ref.dtype)
    width = QK_NOPE + V_DIM
    for h in range(MLA_HEADS):
        base = h * width
        k_ref[h, :, :QK_NOPE] = kv[:, base:base + QK_NOPE].astype(k_ref.dtype)
        k_ref[h, :, QK_NOPE:] = k_rope
        vt_ref[h, :V_DIM, :] = kv[:, base + QK_NOPE:base + width].T.astype(vt_ref.dtype)
        vt_ref[h, V_DIM:, :] = ones


def kv_up(qkv_a, gains, li, w_ukv, cos, sin, name):
    m = qkv_a.shape[0]
    tr = HEAD_ROWS
    rope_blk = (Q_LORA + KV_LORA) // V7X_LANES
    return pl.pallas_call(
        _kv_up_kernel,
        grid=(m // tr,),
        in_specs=[pl.BlockSpec((tr, KV_LORA), lambda i: (i, 1)),
                  pl.BlockSpec((tr, V7X_LANES), lambda i: (i, rope_blk)),
                  _gain_spec(li, KV_LORA, 1),
                  pl.BlockSpec((1, KV_LORA, MLA_HEADS * (QK_NOPE + V_DIM)), lambda i: (li, 0, 0)),
                  pl.BlockSpec((tr, V7X_LANES), lambda i: (i, 0)),
                  pl.BlockSpec((tr, V7X_LANES), lambda i: (i, 0))],
        out_specs=[pl.BlockSpec((MLA_HEADS, tr, HEAD_PAD), lambda i: (0, i, 0)),
                   pl.BlockSpec((MLA_HEADS, VT_ROWS, tr), lambda i: (0, 0, i))],
        out_shape=[jax.ShapeDtypeStruct((MLA_HEADS, m, HEAD_PAD), BF16),
                   jax.ShapeDtypeStruct((MLA_HEADS, VT_ROWS, m), BF16)],
        compiler_params=_params(("parallel",), 40),
        name=name,
    )(qkv_a, qkv_a, gains, w_ukv, cos, sin)


def _flash_kernel(piece, n_lat, has_ctx, *refs):
    if has_ctx:
        qt_ref, k_ref, vt_ref, kc_ref, vtc_ref, o_ref = refs
    else:
        qt_ref, k_ref, vt_ref, o_ref = refs
    qt = qt_ref[0]
    tq = qt.shape[1]
    n_all = n_lat + int(has_ctx)

    def keys(i):
        return kc_ref[0] if i >= n_lat else k_ref[0, i * piece:(i + 1) * piece, :]

    def values_t(i):
        return vtc_ref[0] if i >= n_lat else vt_ref[0, :, i * piece:(i + 1) * piece]

    m = jnp.full((1, tq), -jnp.inf, F32)
    acc = jnp.zeros((VT_ROWS, tq), F32)
    pending = [_dot(keys(i), qt) for i in range(min(ATTN_AHEAD, n_all))]
    for t in range(n_all):
        s = pending.pop(0)
        if t + ATTN_AHEAD < n_all:
            pending.append(_dot(keys(t + ATTN_AHEAD), qt))
        m_new = jnp.maximum(m, jnp.max(s, axis=0, keepdims=True))
        p = jnp.exp2(s - m_new).astype(BF16)
        acc = jnp.exp2(m - m_new) * acc + _dot(values_t(t), p)
        m = m_new
    o_ref[...] = (acc[:V_DIM] / acc[V_DIM:V_DIM + 1]).T.astype(o_ref.dtype)


def flash_attention(qt, k, vt, k_ctx, vt_ctx, name):
    h, _, m = qt.shape
    n_k = k.shape[1]
    tq = min(m, ATTN_TQ)
    piece = min(n_k, ATTN_PIECE)
    has_ctx = k_ctx is not None
    in_specs = [pl.BlockSpec((1, HEAD_PAD, tq), lambda hh, i: (hh, 0, i)),
                pl.BlockSpec((1, n_k, HEAD_PAD), lambda hh, i: (hh, 0, 0)),
                pl.BlockSpec((1, VT_ROWS, n_k), lambda hh, i: (hh, 0, 0))]
    args = [qt, k, vt]
    if has_ctx:
        n_c = k_ctx.shape[1]
        in_specs += [pl.BlockSpec((1, n_c, HEAD_PAD), lambda hh, i: (hh, 0, 0)),
                     pl.BlockSpec((1, VT_ROWS, n_c), lambda hh, i: (hh, 0, 0))]
        args += [k_ctx, vt_ctx]
    return pl.pallas_call(
        functools.partial(_flash_kernel, piece, n_k // piece, has_ctx),
        grid=(h, m // tq),
        in_specs=in_specs,
        out_specs=pl.BlockSpec((tq, V_DIM), lambda hh, i: (i, hh)),
        out_shape=jax.ShapeDtypeStruct((m, h * V_DIM), BF16),
        compiler_params=_params(("parallel", "parallel"), 48),
        name=name,
    )(*args)


def _swap_pairs(w):
    q = QK_ROPE // 4
    r1, r2, c1, c2 = (w[..., j * q:(j + 1) * q] for j in range(4))
    return jnp.concatenate([-r2, r1, -c2, c1], axis=-1)


def _rope_tables(n):
    q = QK_ROPE // 4
    rows = n // GRID_W
    inv = jnp.power(jnp.float32(ROPE_THETA), -jnp.arange(q, dtype=F32) / q)
    ar = jnp.arange(rows).astype(F32)[:, None] * inv
    ac = jnp.arange(GRID_W).astype(F32)[:, None] * inv
    cos_r, sin_r = (jnp.repeat(f(ar), GRID_W, axis=0) for f in (jnp.cos, jnp.sin))
    cos_c, sin_c = (jnp.tile(f(ac), (rows, 1)) for f in (jnp.cos, jnp.sin))
    zeros = jnp.zeros((n, V7X_LANES - QK_ROPE), F32)
    cos = jnp.concatenate([cos_r, cos_r, cos_c, cos_c, zeros], axis=1)
    sin = jnp.concatenate([sin_r, sin_r, sin_c, sin_c, zeros], axis=1)
    return cos, sin


def _identity_tables(n):
    cos = jnp.concatenate([jnp.ones((n, QK_ROPE), F32), jnp.zeros((n, V7X_LANES - QK_ROPE), F32)], axis=1)
    return cos, jnp.zeros((n, V7X_LANES), F32)


def _mla_weights(w_dq, w_uq, w_dkv):
    n = w_dq.shape[0]
    rope = w_dkv[..., KV_LORA:]
    w_down = jnp.concatenate([w_dq, w_dkv[..., :KV_LORA], rope, _swap_pairs(rope)], axis=-1)
    wq = w_uq.reshape(n, Q_LORA, MLA_HEADS, QK_NOPE + QK_ROPE)
    w_up_q = jnp.concatenate([wq, _swap_pairs(wq[..., QK_NOPE:])], axis=-1)
    w_up_q_t = w_up_q.reshape(n, Q_LORA, MLA_HEADS * HEAD_PAD).transpose(0, 2, 1)
    return w_down, w_up_q_t.astype(BF16)


def kernel(x, c, ctx, c_ctx, ada_w, ada_b, norm1_g, norm2_g, even_w_in, pool_w, pool_scale, conv_w,
           even_w_out, mla_w_dq, mla_q_norm_g, mla_w_uq, mla_w_dkv, mla_kv_norm_g, mla_w_ukv, mla_w_o,
           ffn_w_gate, ffn_w_up, ffn_w_down, final_norm_g):
    n_lat, n_ctx = x.shape[1], ctx.shape[1]
    x_lat, x_ctx = x[0], ctx[0]

    cond8 = jnp.concatenate([c, c_ctx[None, :], jnp.zeros((V7X_SUBLANES - 2, D_MODEL), F32)], axis=0)
    mods = ada_table(cond8, ada_w, ada_b)
    g1 = norm1_g.reshape(DEPTH, 1, D_MODEL)
    g2 = norm2_g.reshape(DEPTH, 1, D_MODEL)
    gq = mla_q_norm_g.reshape(-1, 1, Q_LORA)
    gkv = mla_kv_norm_g.reshape(-1, 1, KV_LORA)
    final_g = final_norm_g.reshape(1, 1, D_MODEL)
    rope_lat = _rope_tables(n_lat)
    rope_ctx = _identity_tables(n_ctx)

    w_in, pw = even_w_in.astype(BF16), pool_w.astype(BF16)
    ps = pool_scale.reshape(-1, 1, POOL_WIDTH)
    rope_t = {"lat": tuple(t[:, :QK_ROPE].T for t in rope_lat), "ctx": tuple(t[:, :QK_ROPE].T for t in rope_ctx)}
    w_down, w_up_q_t = _mla_weights(mla_w_dq, mla_w_uq, mla_w_dkv)
    w_ukv = mla_w_ukv.astype(BF16)

    state = {"lat": (x_lat, normmod(x_lat, g1, mods, 0, LAT_ROW, 0, "norm_in_lat")),
             "ctx": (x_ctx, normmod(x_ctx, g1, mods, 0, CTX_ROW, 0, "norm_in_ctx"))}

    for layer in range(DEPTH):
        last = layer == DEPTH - 1
        odd = layer % 2 == 1
        li = layer // 2
        tags = ["lat"] + (["ctx"] if odd or not last else [])
        branch = {}

        if odd:
            proj = {}
            for tag in tags:
                cos, sin = rope_lat if tag == "lat" else rope_ctx
                a = wcast_matmul(state[tag][1], w_down, li, f"mla_down_{tag}")
                k, vt = kv_up(a, gkv, li, w_ukv, cos, sin, f"kv_up_{tag}")
                need_q = tag == "lat" or not last
                qt = q_up(a, gq, li, w_up_q_t, *rope_t[tag], f"q_up_{tag}") if need_q else None
                proj[tag] = (qt, k, vt)
            qt, k, vt = proj["lat"]
            qtc, kc, vtc = proj["ctx"]
            branch["lat"] = flash_attention(qt, k, vt, kc, vtc, "attn_lat")
            if not last:
                branch["ctx"] = flash_attention(qtc, kc, vtc, None, None, "attn_ctx")
            w_proj, proj_name = mla_w_o, "attn_out"
        else:
            for tag in tags:
                z = matmul_cols(state[tag][1], w_in, li, 512, f"even_in_{tag}")
                branch[tag] = even_mid(z, pw, ps, conv_w, li, f"even_mid_{tag}")
            w_proj, proj_name = even_w_out, "even_out"

        proj_in = {tag: (y, state[tag][0]) for tag, y in branch.items()}
        ffn_in = matmul_residual(proj_in["lat"], proj_in.get("ctx"), w_proj, li, mods, layer, g2, proj_name)
        state.update(ffn_residual(ffn_in["lat"], ffn_in.get("ctx"), mods, layer, ffn_w_gate, ffn_w_up, ffn_w_down,
                                  final_g if last else g1, last, "ffn"))

    return state["lat"][0][None]
```

```python
import functools

import jax
import jax.numpy as jnp
import numpy as np
from jax import lax
from jax.experimental import pallas as pl
from jax.experimental.pallas import tpu as pltpu

D_MODEL = 2048
DEPTH = 4
GRID_W = 64
EPS = 1e-6
POOL_WINDOWS = (2, 4, 8, 16)
POOL_WIDTH = 1024
POOL_GROUP = 256
CONV_WIDTH = 1024
EVEN_IN = POOL_WIDTH + 3 * CONV_WIDTH
MLA_HEADS = 16
QK_NOPE = 128
QK_ROPE = 64
V_DIM = 128
Q_LORA = 512
KV_LORA = 512
ROPE_THETA = 10000.0
D_FF = 5632

V7X_LANES = 128
V7X_SUBLANES = 8
V7X_MXU_DIM = 256

HEAD_PAD = V7X_MXU_DIM
HALO = V7X_SUBLANES
LAT_ROW, CTX_ROW = 0, 1
BF16_SUBLANES = 2 * V7X_SUBLANES
VT_ROWS = V_DIM + BF16_SUBLANES
ATTN_TQ = 2048
ATTN_PIECE = 2 * V7X_MXU_DIM
ATTN_AHEAD = 2
HEAD_ROWS = 256
BF16 = jnp.bfloat16
F32 = jnp.float32


def _params(semantics, vmem_mb):
    return pltpu.CompilerParams(dimension_semantics=semantics,
                                vmem_limit_bytes=vmem_mb * 1024 * 1024)


def _tile_rows(m):
    return min(m, 1024)


def _dot(a, b):
    return jnp.dot(a, b, preferred_element_type=F32)


def _rms(x, g):
    return x * lax.rsqrt(jnp.mean(x * x, axis=-1, keepdims=True) + EPS) * g


def _normmod(x, g_ref, sh_ref, sc_ref, row):
    inv = lax.rsqrt(jnp.mean(x * x, axis=-1, keepdims=True) + EPS)
    gain = g_ref[0] * (1.0 + sc_ref[0, row:row + 1, :])
    return (x * inv * gain + sh_ref[0, row:row + 1, :]).astype(BF16)


def _mod_spec(layer, chunk, ngrid):
    if ngrid == 1:
        return pl.BlockSpec((1, V7X_SUBLANES, D_MODEL), lambda i: (layer, 0, chunk))
    return pl.BlockSpec((1, V7X_SUBLANES, D_MODEL), lambda i, j: (layer, 0, chunk))


def _gain_spec(layer, width, ngrid):
    if ngrid == 1:
        return pl.BlockSpec((1, 1, width), lambda i: (layer, 0, 0))
    return pl.BlockSpec((1, 1, width), lambda i, j: (layer, 0, 0))


def _ada_kernel(cond_ref, w_ref, b_ref, o_ref):
    c = cond_ref[...]
    s = (c * jax.nn.sigmoid(c)).astype(BF16)
    o_ref[0] = _dot(s, w_ref[0].astype(BF16)) + b_ref[0]


def ada_table(cond8, ada_w, ada_b):
    tn = 1024
    n_out = 6 * D_MODEL
    return pl.pallas_call(
        _ada_kernel,
        grid=(DEPTH, n_out // tn),
        in_specs=[pl.BlockSpec((V7X_SUBLANES, D_MODEL), lambda l, j: (0, 0)),
                  pl.BlockSpec((1, D_MODEL, tn), lambda l, j: (l, 0, j)),
                  pl.BlockSpec((1, 1, tn), lambda l, j: (l, 0, j))],
        out_specs=pl.BlockSpec((1, V7X_SUBLANES, tn), lambda l, j: (l, 0, j)),
        out_shape=jax.ShapeDtypeStruct((DEPTH, V7X_SUBLANES, n_out), F32),
        compiler_params=_params(("parallel", "parallel"), 40),
        name="ada_table",
    )(cond8, ada_w, ada_b.reshape(DEPTH, 1, n_out))


def _matmul_cols_kernel(h_ref, w_ref, o_ref):
    o_ref[...] = _dot(h_ref[...], w_ref[0])


def matmul_cols(h, w, li, tn, name):
    m, k = h.shape
    tm = _tile_rows(m)
    n_out = w.shape[2]
    return pl.pallas_call(
        _matmul_cols_kernel,
        grid=(m // tm, n_out // tn),
        in_specs=[pl.BlockSpec((tm, k), lambda i, j: (i, 0)),
                  pl.BlockSpec((1, k, tn), lambda i, j: (li, 0, j))],
        out_specs=pl.BlockSpec((tm, tn), lambda i, j: (i, j)),
        out_shape=jax.ShapeDtypeStruct((m, n_out), F32),
        compiler_params=_params(("parallel", "parallel"), 40),
        name=name,
    )(h, w)


def _normmod_kernel(row, x_ref, g_ref, sh_ref, sc_ref, h_ref):
    h_ref[...] = _normmod(x_ref[...], g_ref, sh_ref, sc_ref, row)


def normmod(x, gains, mods, layer, row, chunk, name):
    m = x.shape[0]
    tm = min(m, 512)
    return pl.pallas_call(
        functools.partial(_normmod_kernel, row),
        grid=(m // tm,),
        in_specs=[pl.BlockSpec((tm, D_MODEL), lambda i: (i, 0)),
                  _gain_spec(layer, D_MODEL, 1),
                  _mod_spec(layer, chunk, 1),
                  _mod_spec(layer, chunk + 1, 1)],
        out_specs=pl.BlockSpec((tm, D_MODEL), lambda i: (i, 0)),
        out_shape=jax.ShapeDtypeStruct((m, D_MODEL), BF16),
        compiler_params=_params(("parallel",), 32),
        name=name,
    )(x, gains, mods, mods)


def _round_weights(pairs):
    @pl.when(pl.program_id(1) == 0)
    def _():
        for w_ref, w_s in pairs:
            w_s[...] = w_ref[0].astype(BF16)


def _wcast_matmul_kernel(h_ref, w_ref, o_ref, w_s):
    _round_weights([(w_ref, w_s)])
    o_ref[...] = _dot(h_ref[...], w_s[...])


def wcast_matmul(h, w, li, name):
    m, k = h.shape
    n_out = w.shape[2]
    tm = _tile_rows(m)
    return pl.pallas_call(
        _wcast_matmul_kernel,
        grid=(1, m // tm),
        in_specs=[pl.BlockSpec((tm, k), lambda j, i: (i, 0)),
                  pl.BlockSpec((1, k, n_out), lambda j, i: (li, 0, 0), pipeline_mode=pl.Buffered(1))],
        out_specs=pl.BlockSpec((tm, n_out), lambda j, i: (i, 0)),
        out_shape=jax.ShapeDtypeStruct((m, n_out), F32),
        scratch_shapes=[pltpu.VMEM((k, n_out), BF16)],
        compiler_params=_params(("parallel", "arbitrary"), 48),
        name=name,
    )(h, w)


def _matmul_residual_kernel(n_lat, with_ctx, a_ref, w_ref, x_ref, gate_ref, g_ref, sh_ref, sc_ref, *rest):
    def rows(a, x, row, o_ref, h_ref):
        out = x[...] + gate_ref[0, row:row + 1, :] * _dot(a[...], w_s[...])
        o_ref[...] = out
        h_ref[...] = _normmod(out, g_ref, sh_ref, sc_ref, row)

    if with_ctx:
        ac_ref, xc_ref, o_ref, h_ref, oc_ref, hc_ref, w_s = rest
    else:
        o_ref, h_ref, w_s = rest
    _round_weights([(w_ref, w_s)])
    rows(a_ref, x_ref, LAT_ROW, o_ref, h_ref)
    if with_ctx:
        pl.when(pl.program_id(1) == n_lat - 1)(lambda: rows(ac_ref, xc_ref, CTX_ROW, oc_ref, hc_ref))


def matmul_residual(lat, ctx, w, li, mods, layer, gains2, name):
    a, x = lat
    m, k = a.shape
    tm = HEAD_ROWS
    with_ctx = ctx is not None
    row_spec = pl.BlockSpec((tm, D_MODEL), lambda j, i: (i, 0))
    in_specs = [pl.BlockSpec((tm, k), lambda j, i: (i, 0)),
                pl.BlockSpec((1, k, D_MODEL), lambda j, i: (li, 0, 0), pipeline_mode=pl.Buffered(1)),
                row_spec,
                _mod_spec(layer, 2, 2),
                _gain_spec(layer, D_MODEL, 2),
                _mod_spec(layer, 3, 2),
                _mod_spec(layer, 4, 2)]
    out_specs = [row_spec, row_spec]
    out_shape = [jax.ShapeDtypeStruct((m, D_MODEL), F32), jax.ShapeDtypeStruct((m, D_MODEL), BF16)]
    args = [a, w, x, mods, gains2, mods, mods]
    if with_ctx:
        ac, xc = ctx
        mc = xc.shape[0]
        whole = lambda cols: pl.BlockSpec((mc, cols), lambda j, i: (0, 0), pipeline_mode=pl.Buffered(1))
        in_specs += [whole(k), whole(D_MODEL)]
        args += [ac, xc]
        out_specs += [pl.BlockSpec((mc, D_MODEL), lambda j, i: (0, 0))] * 2
        out_shape += [jax.ShapeDtypeStruct((mc, D_MODEL), F32), jax.ShapeDtypeStruct((mc, D_MODEL), BF16)]
    outs = pl.pallas_call(
        functools.partial(_matmul_residual_kernel, m // tm, with_ctx),
        grid=(1, m // tm),
        in_specs=in_specs,
        out_specs=out_specs,
        out_shape=out_shape,
        scratch_shapes=[pltpu.VMEM((k, D_MODEL), BF16)],
        compiler_params=_params(("parallel", "arbitrary"), 56),
        name=name,
    )(*args)
    result = {"lat": (outs[1], outs[0])}
    if with_ctx:
        result["ctx"] = (outs[3], outs[2])
    return result


def _ffn_up_kernel(n_lat, with_ctx, *refs):
    if with_ctx:
        h_ref, wg_ref, wu_ref, wd_ref, hc_ref, a_ref, wdb_ref, ac_ref, wg_s, wu_s = refs
    else:
        h_ref, wg_ref, wu_ref, wd_ref, a_ref, wdb_ref, wg_s, wu_s = refs
    _round_weights([(wg_ref, wg_s), (wu_ref, wu_s), (wd_ref, wdb_ref)])

    def swiglu(src_ref, dst_ref):
        h = src_ref[...]
        g = _dot(h, wg_s[...])
        u = _dot(h, wu_s[...])
        dst_ref[...] = (g * jax.nn.sigmoid(g) * u).astype(dst_ref.dtype)

    swiglu(h_ref, a_ref)
    if with_ctx:
        pl.when(pl.program_id(1) == n_lat - 1)(lambda: swiglu(hc_ref, ac_ref))


def _ffn_down_kernel(n_lat, final_norm, with_ctx, a_ref, wd_ref, x_ref, gate_ref, g_ref, sh_ref, sc_ref, *rest):
    def rows(a, x, row, o_ref, h_ref):
        out = x[...] + gate_ref[0, row:row + 1, :] * _dot(a[...], wd_ref[...])
        if final_norm:
            o_ref[...] = _rms(out, g_ref[0])
        else:
            o_ref[...] = out
            h_ref[...] = _normmod(out, g_ref, sh_ref, sc_ref, row)

    if with_ctx:
        ac_ref, xc_ref, o_ref, h_ref, oc_ref, hc_ref = rest
        rows(a_ref, x_ref, LAT_ROW, o_ref, h_ref)
        pl.when(pl.program_id(0) == n_lat - 1)(lambda: rows(ac_ref, xc_ref, CTX_ROW, oc_ref, hc_ref))
    else:
        rows(a_ref, x_ref, LAT_ROW, rest[0], None if final_norm else rest[1])


def ffn_residual(lat, ctx, mods, layer, wg, wu, wd, next_gains, final_norm, name):
    h, x = lat
    m = x.shape[0]
    tm = _tile_rows(m)
    tf = 512
    with_ctx = ctx is not None
    n_up = m // tm
    in_specs = [pl.BlockSpec((tm, D_MODEL), lambda f, i: (i, 0)),
                pl.BlockSpec((1, D_MODEL, tf), lambda f, i: (layer, 0, f)),
                pl.BlockSpec((1, D_MODEL, tf), lambda f, i: (layer, 0, f)),
                pl.BlockSpec((1, tf, D_MODEL), lambda f, i: (layer, f, 0))]
    out_specs = [pl.BlockSpec((tm, tf), lambda f, i: (i, f)),
                 pl.BlockSpec((tf, D_MODEL), lambda f, i: (f, 0))]
    out_shape = [jax.ShapeDtypeStruct((m, D_FF), BF16), jax.ShapeDtypeStruct((D_FF, D_MODEL), BF16)]
    args = [h, wg, wu, wd]
    if with_ctx:
        hc, xc = ctx
        mc = xc.shape[0]
        in_specs.append(pl.BlockSpec((mc, D_MODEL), lambda f, i: (0, 0)))
        out_specs.append(pl.BlockSpec((mc, tf), lambda f, i: (0, f)))
        out_shape.append(jax.ShapeDtypeStruct((mc, D_FF), BF16))
        args.append(hc)
    acts = pl.pallas_call(
        functools.partial(_ffn_up_kernel, n_up, with_ctx),
        grid=(D_FF // tf, n_up),
        in_specs=in_specs,
        out_specs=out_specs,
        out_shape=out_shape,
        scratch_shapes=[pltpu.VMEM((D_MODEL, tf), BF16), pltpu.VMEM((D_MODEL, tf), BF16)],
        compiler_params=_params(("parallel", "arbitrary"), 56),
        name=name + "_up",
    )(*args)

    tr = HEAD_ROWS
    n_dn = m // tr
    nxt = 0 if final_norm else layer + 1
    row_spec = pl.BlockSpec((tr, D_MODEL), lambda i: (i, 0))
    in_specs = [pl.BlockSpec((tr, D_FF), lambda i: (i, 0)),
                pl.BlockSpec((D_FF, D_MODEL), lambda i: (0, 0), pipeline_mode=pl.Buffered(1)),
                row_spec,
                _mod_spec(layer, 5, 1),
                _gain_spec(nxt, D_MODEL, 1),
                _mod_spec(nxt, 0, 1),
                _mod_spec(nxt, 1, 1)]
    out_specs = [row_spec]
    out_shape = [jax.ShapeDtypeStruct((m, D_MODEL), F32)]
    args = [acts[0], acts[1], x, mods, next_gains, mods, mods]
    if not final_norm:
        out_specs.append(row_spec)
        out_shape.append(jax.ShapeDtypeStruct((m, D_MODEL), BF16))
    if with_ctx:
        whole = lambda cols: pl.BlockSpec((mc, cols), lambda i: (0, 0), pipeline_mode=pl.Buffered(1))
        in_specs += [whole(D_FF), whole(D_MODEL)]
        args += [acts[2], xc]
        out_specs += [pl.BlockSpec((mc, D_MODEL), lambda i: (0, 0))] * 2
        out_shape += [jax.ShapeDtypeStruct((mc, D_MODEL), F32), jax.ShapeDtypeStruct((mc, D_MODEL), BF16)]
    outs = pl.pallas_call(
        functools.partial(_ffn_down_kernel, n_dn, final_norm, with_ctx),
        grid=(n_dn,),
        in_specs=in_specs,
        out_specs=out_specs,
        out_shape=out_shape,
        compiler_params=_params(("arbitrary",), 56),
        name=name + "_down",
    )(*args)
    if final_norm:
        return {"lat": (outs[0], None)}
    state = {"lat": (outs[0], outs[1])}
    if with_ctx:
        state["ctx"] = (outs[2], outs[3])
    return state


def _shift_rows(x, k):
    n = x.shape[0]
    return pltpu.roll(x, (n - k) % n, axis=0)


def _even_mid_kernel(seq_len, zp_ref, z_ref, zn_ref, pw_ref, ps_ref, cw_ref, o_ref):
    i = pl.program_id(0)
    tm = z_ref.shape[0]
    first = i == 0
    last = i == pl.num_programs(0) - 1
    row = i * tm + lax.broadcasted_iota(jnp.int32, (tm, 1), 0)

    def with_halo(lo, hi):
        before = jnp.where(first, 0.0, zp_ref[:, lo:hi])
        after = jnp.where(last, 0.0, zn_ref[:, lo:hi])
        return jnp.concatenate([before, z_ref[:, lo:hi], after], axis=0)

    for g, w in enumerate(POOL_WINDOWS):
        lo_c, hi_c = g * POOL_GROUP, (g + 1) * POOL_GROUP
        u = with_halo(lo_c, hi_c)
        b = u
        s = 1
        while s < w:
            b = b + _shift_rows(b, s)
            s *= 2
        win = _shift_rows(b, -(w // 2))[HALO:HALO + tm]
        cnt = (jnp.minimum(row + (w - w // 2), seq_len) - jnp.maximum(row - w // 2, 0)).astype(F32)
        p = win / cnt - z_ref[:, lo_c:hi_c]
        y = _dot(p.astype(BF16), pw_ref[0, g]) * ps_ref[0, :, lo_c:hi_c]
        o_ref[:, lo_c:hi_c] = y.astype(o_ref.dtype)

    c0 = POOL_WIDTH
    gate_b = z_ref[:, c0:c0 + CONV_WIDTH]
    u = with_halo(c0 + CONV_WIDTH, c0 + 2 * CONV_WIDTH) * with_halo(c0 + 2 * CONV_WIDTH, c0 + 3 * CONV_WIDTH)
    conv = (_shift_rows(u, -1) * cw_ref[0, 0:1, :] + u * cw_ref[0, 1:2, :] + _shift_rows(u, 1) * cw_ref[0, 2:3, :])
    y_b = gate_b * conv[HALO:HALO + tm]
    o_ref[:, POOL_WIDTH:] = y_b.astype(o_ref.dtype)


def even_mid(z, pool_w, pool_scale, conv_w, li, name):
    m = z.shape[0]
    tm = min(m, 512)
    nb = tm // HALO
    last_blk = m // HALO - 1
    n_win = len(POOL_WINDOWS)
    return pl.pallas_call(
        functools.partial(_even_mid_kernel, m),
        grid=(m // tm,),
        in_specs=[pl.BlockSpec((HALO, EVEN_IN), lambda i: (jnp.maximum(i * nb - 1, 0), 0)),
                  pl.BlockSpec((tm, EVEN_IN), lambda i: (i, 0)),
                  pl.BlockSpec((HALO, EVEN_IN), lambda i: (jnp.minimum((i + 1) * nb, last_blk), 0)),
                  pl.BlockSpec((1, n_win, POOL_GROUP, POOL_GROUP), lambda i: (li, 0, 0, 0)),
                  pl.BlockSpec((1, 1, POOL_WIDTH), lambda i: (li, 0, 0)),
                  pl.BlockSpec((1, 3, CONV_WIDTH), lambda i: (li, 0, 0))],
        out_specs=pl.BlockSpec((tm, POOL_WIDTH + CONV_WIDTH), lambda i: (i, 0)),
        out_shape=jax.ShapeDtypeStruct((m, POOL_WIDTH + CONV_WIDTH), BF16),
        compiler_params=_params(("parallel",), 48),
        name=name,
    )(z, z, z, pool_w, pool_scale, conv_w)


def _rope_lanes(t, cos_ref, sin_ref):
    return t * cos_ref[...] + pltpu.roll(t, QK_ROPE, axis=1) * sin_ref[...]


def _q_up_kernel(scale, a_ref, g_ref, wt_ref, cos_t_ref, sin_t_ref, qt_ref):
    n_t = _rms(a_ref[...], g_ref[0]).T.astype(BF16)
    cos_t, sin_t = cos_t_ref[...] * scale, sin_t_ref[...] * scale
    rope_end = QK_NOPE + QK_ROPE
    zeros = jnp.zeros((HEAD_PAD - rope_end, n_t.shape[1]), qt_ref.dtype)
    for h in range(MLA_HEADS):
        q_t = _dot(wt_ref[0, h * HEAD_PAD:(h + 1) * HEAD_PAD, :], n_t)
        qt_ref[h, :QK_NOPE, :] = (q_t[:QK_NOPE] * scale).astype(qt_ref.dtype)
        rope = q_t[QK_NOPE:rope_end] * cos_t + q_t[rope_end:] * sin_t
        qt_ref[h, QK_NOPE:rope_end, :] = rope.astype(qt_ref.dtype)
        qt_ref[h, rope_end:, :] = zeros


def q_up(qkv_a, gains, li, w_uq_t, cos_t, sin_t, name):
    m = qkv_a.shape[0]
    tr = HEAD_ROWS
    scale = float((QK_NOPE + QK_ROPE) ** -0.5 * np.log2(np.e))
    return pl.pallas_call(
        functools.partial(_q_up_kernel, scale),
        grid=(m // tr,),
        in_specs=[pl.BlockSpec((tr, Q_LORA), lambda i: (i, 0)),
                  _gain_spec(li, Q_LORA, 1),
                  pl.BlockSpec((1, MLA_HEADS * HEAD_PAD, Q_LORA), lambda i: (li, 0, 0)),
                  pl.BlockSpec((QK_ROPE, tr), lambda i: (0, i)),
                  pl.BlockSpec((QK_ROPE, tr), lambda i: (0, i))],
        out_specs=pl.BlockSpec((MLA_HEADS, HEAD_PAD, tr), lambda i: (0, 0, i)),
        out_shape=jax.ShapeDtypeStruct((MLA_HEADS, HEAD_PAD, m), BF16),
        compiler_params=_params(("parallel",), 40),
        name=name,
    )(qkv_a, gains, w_uq_t, cos_t, sin_t)


def _kv_up_kernel(c_ref, r_ref, g_ref, w_ref, cos_ref, sin_ref, k_ref, vt_ref):
    n = _rms(c_ref[...], g_ref[0]).astype(BF16)
    k_rope = _rope_lanes(r_ref[...], cos_ref, sin_ref).astype(k_ref.dtype)
    kv = _dot(n, w_ref[0])
    ones = jnp.ones((VT_ROWS - V_DIM, vt_ref.shape[2]), vt_ref.dtype)
    width = QK_NOPE + V_DIM
    for h in range(MLA_HEADS):
        base = h * width
        k_ref[h, :, :QK_NOPE] = kv[:, base:base + QK_NOPE].astype(k_ref.dtype)
        k_ref[h, :, QK_NOPE:] = k_rope
        vt_ref[h, :V_DIM, :] = kv[:, base + QK_NOPE:base + width].T.astype(vt_ref.dtype)
        vt_ref[h, V_DIM:, :] = ones


def kv_up(qkv_a, gains, li, w_ukv, cos, sin, name):
    m = qkv_a.shape[0]
    tr = HEAD_ROWS
    rope_blk = (Q_LORA + KV_LORA) // V7X_LANES
    return pl.pallas_call(
        _kv_up_kernel,
        grid=(m // tr,),
        in_specs=[pl.BlockSpec((tr, KV_LORA), lambda i: (i, 1)),
                  pl.BlockSpec((tr, V7X_LANES), lambda i: (i, rope_blk)),
                  _gain_spec(li, KV_LORA, 1),
                  pl.BlockSpec((1, KV_LORA, MLA_HEADS * (QK_NOPE + V_DIM)), lambda i: (li, 0, 0)),
                  pl.BlockSpec((tr, V7X_LANES), lambda i: (i, 0)),
                  pl.BlockSpec((tr, V7X_LANES), lambda i: (i, 0))],
        out_specs=[pl.BlockSpec((MLA_HEADS, tr, HEAD_PAD), lambda i: (0, i, 0)),
                   pl.BlockSpec((MLA_HEADS, VT_ROWS, tr), lambda i: (0, 0, i))],
        out_shape=[jax.ShapeDtypeStruct((MLA_HEADS, m, HEAD_PAD), BF16),
                   jax.ShapeDtypeStruct((MLA_HEADS, VT_ROWS, m), BF16)],
        compiler_params=_params(("parallel",), 40),
        name=name,
    )(qkv_a, qkv_a, gains, w_ukv, cos, sin)


def _flash_kernel(piece, n_lat, has_ctx, *refs):
    if has_ctx:
        qt_ref, k_ref, vt_ref, kc_ref, vtc_ref, o_ref = refs
    else:
        qt_ref, k_ref, vt_ref, o_ref = refs
    qt = qt_ref[0]
    tq = qt.shape[1]
    n_all = n_lat + int(has_ctx)

    def keys(i):
        return kc_ref[0] if i >= n_lat else k_ref[0, i * piece:(i + 1) * piece, :]

    def values_t(i):
        return vtc_ref[0] if i >= n_lat else vt_ref[0, :, i * piece:(i + 1) * piece]

    m = jnp.full((1, tq), -jnp.inf, F32)
    acc = jnp.zeros((VT_ROWS, tq), F32)
    pending = [_dot(keys(i), qt) for i in range(min(ATTN_AHEAD, n_all))]
    for t in range(n_all):
        s = pending.pop(0)
        if t + ATTN_AHEAD < n_all:
            pending.append(_dot(keys(t + ATTN_AHEAD), qt))
        m_new = jnp.maximum(m, jnp.max(s, axis=0, keepdims=True))
        p = jnp.exp2(s - m_new).astype(BF16)
        acc = jnp.exp2(m - m_new) * acc + _dot(values_t(t), p)
        m = m_new
    o_ref[...] = (acc[:V_DIM] / acc[V_DIM:V_DIM + 1]).T.astype(o_ref.dtype)


def flash_attention(qt, k, vt, k_ctx, vt_ctx, name):
    h, _, m = qt.shape
    n_k = k.shape[1]
    tq = min(m, ATTN_TQ)
    piece = min(n_k, ATTN_PIECE)
    has_ctx = k_ctx is not None
    in_specs = [pl.BlockSpec((1, HEAD_PAD, tq), lambda hh, i: (hh, 0, i)),
                pl.BlockSpec((1, n_k, HEAD_PAD), lambda hh, i: (hh, 0, 0)),
                pl.BlockSpec((1, VT_ROWS, n_k), lambda hh, i: (hh, 0, 0))]
    args = [qt, k, vt]
    if has_ctx:
        n_c = k_ctx.shape[1]
        in_specs += [pl.BlockSpec((1, n_c, HEAD_PAD), lambda hh, i: (hh, 0, 0)),
                     pl.BlockSpec((1, VT_ROWS, n_c), lambda hh, i: (hh, 0, 0))]
        args += [k_ctx, vt_ctx]
    return pl.pallas_call(
        functools.partial(_flash_kernel, piece, n_k // piece, has_ctx),
        grid=(h, m // tq),
        in_specs=in_specs,
        out_specs=pl.BlockSpec((tq, V_DIM), lambda hh, i: (i, hh)),
        out_shape=jax.ShapeDtypeStruct((m, h * V_DIM), BF16),
        compiler_params=_params(("parallel", "parallel"), 48),
        name=name,
    )(*args)


def _swap_pairs(w):
    q = QK_ROPE // 4
    r1, r2, c1, c2 = (w[..., j * q:(j + 1) * q] for j in range(4))
    return jnp.concatenate([-r2, r1, -c2, c1], axis=-1)


def _rope_tables(n):
    q = QK_ROPE // 4
    rows = n // GRID_W
    inv = jnp.power(jnp.float32(ROPE_THETA), -jnp.arange(q, dtype=F32) / q)
    ar = jnp.arange(rows).astype(F32)[:, None] * inv
    ac = jnp.arange(GRID_W).astype(F32)[:, None] * inv
    cos_r, sin_r = (jnp.repeat(f(ar), GRID_W, axis=0) for f in (jnp.cos, jnp.sin))
    cos_c, sin_c = (jnp.tile(f(ac), (rows, 1)) for f in (jnp.cos, jnp.sin))
    zeros = jnp.zeros((n, V7X_LANES - QK_ROPE), F32)
    cos = jnp.concatenate([cos_r, cos_r, cos_c, cos_c, zeros], axis=1)
    sin = jnp.concatenate([sin_r, sin_r, sin_c, sin_c, zeros], axis=1)
    return cos, sin


def _identity_tables(n):
    cos = jnp.concatenate([jnp.ones((n, QK_ROPE), F32), jnp.zeros((n, V7X_LANES - QK_ROPE), F32)], axis=1)
    return cos, jnp.zeros((n, V7X_LANES), F32)


def _mla_weights(w_dq, w_uq, w_dkv):
    n = w_dq.shape[0]
    rope = w_dkv[..., KV_LORA:]
    w_down = jnp.concatenate([w_dq, w_dkv[..., :KV_LORA], rope, _swap_pairs(rope)], axis=-1)
    wq = w_uq.reshape(n, Q_LORA, MLA_HEADS, QK_NOPE + QK_ROPE)
    w_up_q = jnp.concatenate([wq, _swap_pairs(wq[..., QK_NOPE:])], axis=-1)
    w_up_q_t = w_up_q.reshape(n, Q_LORA, MLA_HEADS * HEAD_PAD).transpose(0, 2, 1)
    return w_down, w_up_q_t.astype(BF16)


def kernel(x, c, ctx, c_ctx, ada_w, ada_b, norm1_g, norm2_g, even_w_in, pool_w, pool_scale, conv_w,
           even_w_out, mla_w_dq, mla_q_norm_g, mla_w_uq, mla_w_dkv, mla_kv_norm_g, mla_w_ukv, mla_w_o,
           ffn_w_gate, ffn_w_up, ffn_w_down, final_norm_g):
    n_lat, n_ctx = x.shape[1], ctx.shape[1]
    x_lat, x_ctx = x[0], ctx[0]

    cond8 = jnp.concatenate([c, c_ctx[None, :], jnp.zeros((V7X_SUBLANES - 2, D_MODEL), F32)], axis=0)
    mods = ada_table(cond8, ada_w, ada_b)
    g1 = norm1_g.reshape(DEPTH, 1, D_MODEL)
    g2 = norm2_g.reshape(DEPTH, 1, D_MODEL)
    gq = mla_q_norm_g.reshape(-1, 1, Q_LORA)
    gkv = mla_kv_norm_g.reshape(-1, 1, KV_LORA)
    final_g = final_norm_g.reshape(1, 1, D_MODEL)
    rope_lat = _rope_tables(n_lat)
    rope_ctx = _identity_tables(n_ctx)

    w_in, pw = even_w_in.astype(BF16), pool_w.astype(BF16)
    ps = pool_scale.reshape(-1, 1, POOL_WIDTH)
    rope_t = {"lat": tuple(t[:, :QK_ROPE].T for t in rope_lat), "ctx": tuple(t[:, :QK_ROPE].T for t in rope_ctx)}
    w_down, w_up_q_t = _mla_weights(mla_w_dq, mla_w_uq, mla_w_dkv)
    w_ukv = mla_w_ukv.astype(BF16)

    state = {"lat": (x_lat, normmod(x_lat, g1, mods, 0, LAT_ROW, 0, "norm_in_lat")),
             "ctx": (x_ctx, normmod(x_ctx, g1, mods, 0, CTX_ROW, 0, "norm_in_ctx"))}

    for layer in range(DEPTH):
        last = layer == DEPTH - 1
        odd = layer % 2 == 1
        li = layer // 2
        tags = ["lat"] + (["ctx"] if odd or not last else [])
        branch = {}

        if odd:
            proj = {}
            for tag in tags:
                cos, sin = rope_lat if tag == "lat" else rope_ctx
                a = wcast_matmul(state[tag][1], w_down, li, f"mla_down_{tag}")
                k, vt = kv_up(a, gkv, li, w_ukv, cos, sin, f"kv_up_{tag}")
                need_q = tag == "lat" or not last
                qt = q_up(a, gq, li, w_up_q_t, *rope_t[tag], f"q_up_{tag}") if need_q else None
                proj[tag] = (qt, k, vt)
            qt, k, vt = proj["lat"]
            qtc, kc, vtc = proj["ctx"]
            branch["lat"] = flash_attention(qt, k, vt, kc, vtc, "attn_lat")
            if not last:
                branch["ctx"] = flash_attention(qtc, kc, vtc, None, None, "attn_ctx")
            w_proj, proj_name = mla_w_o, "attn_out"
        else:
            for tag in tags:
                z = matmul_cols(state[tag][1], w_in, li, 512, f"even_in_{tag}")
                branch[tag] = even_mid(z, pw, ps, conv_w, li, f"even_mid_{tag}")
            w_proj, proj_name = even_w_out, "even_out"

        proj_in = {tag: (y, state[tag][0]) for tag, y in branch.items()}
        ffn_in = matmul_residual(proj_in["lat"], proj_in.get("ctx"), w_proj, li, mods, layer, g2, proj_name)
        state.update(ffn_residual(ffn_in["lat"], ffn_in.get("ctx"), mods, layer, ffn_w_gate, ffn_w_up, ffn_w_down,
                                  final_g if last else g1, last, "ffn"))

    return state["lat"][0][None]
```

```python
import functools

import jax
import jax.numpy as jnp
import numpy as np
from jax import lax
from jax.experimental import pallas as pl
from jax.experimental.pallas import tpu as pltpu

D_MODEL = 2048
DEPTH = 4
GRID_W = 64
EPS = 1e-6
POOL_WINDOWS = (2, 4, 8, 16)
POOL_WIDTH = 1024
POOL_GROUP = 256
CONV_WIDTH = 1024
EVEN_IN = POOL_WIDTH + 3 * CONV_WIDTH
MLA_HEADS = 16
QK_NOPE = 128
QK_ROPE = 64
V_DIM = 128
Q_LORA = 512
KV_LORA = 512
ROPE_THETA = 10000.0
D_FF = 5632

V7X_LANES = 128
V7X_SUBLANES = 8
V7X_MXU_DIM = 256

HEAD_PAD = V7X_MXU_DIM
HALO = V7X_SUBLANES
LAT_ROW, CTX_ROW = 0, 1
BF16_SUBLANES = 2 * V7X_SUBLANES
VT_ROWS = V_DIM + BF16_SUBLANES
ATTN_TQ = 1024
ATTN_PIECE = 2 * V7X_MXU_DIM
ATTN_AHEAD = 2
HEAD_ROWS = 256
BF16 = jnp.bfloat16
F32 = jnp.float32


def _params(semantics, vmem_mb):
    return pltpu.CompilerParams(dimension_semantics=semantics,
                                vmem_limit_bytes=vmem_mb * 1024 * 1024)


def _tile_rows(m):
    return min(m, 1024)


def _dot(a, b):
    return jnp.dot(a, b, preferred_element_type=F32)


def _rms(x, g):
    return x * lax.rsqrt(jnp.mean(x * x, axis=-1, keepdims=True) + EPS) * g


def _normmod(x, g_ref, sh_ref, sc_ref, row):
    inv = lax.rsqrt(jnp.mean(x * x, axis=-1, keepdims=True) + EPS)
    gain = g_ref[0] * (1.0 + sc_ref[0, row:row + 1, :])
    return (x * inv * gain + sh_ref[0, row:row + 1, :]).astype(BF16)


def _mod_spec(layer, chunk, ngrid):
    if ngrid == 1:
        return pl.BlockSpec((1, V7X_SUBLANES, D_MODEL), lambda i: (layer, 0, chunk))
    return pl.BlockSpec((1, V7X_SUBLANES, D_MODEL), lambda i, j: (layer, 0, chunk))


def _gain_spec(layer, width, ngrid):
    if ngrid == 1:
        return pl.BlockSpec((1, 1, width), lambda i: (layer, 0, 0))
    return pl.BlockSpec((1, 1, width), lambda i, j: (layer, 0, 0))


def _ada_kernel(cond_ref, w_ref, b_ref, o_ref):
    c = cond_ref[...]
    s = (c * jax.nn.sigmoid(c)).astype(BF16)
    o_ref[0] = _dot(s, w_ref[0].astype(BF16)) + b_ref[0]


def ada_table(cond8, ada_w, ada_b):
    tn = 1024
    n_out = 6 * D_MODEL
    return pl.pallas_call(
        _ada_kernel,
        grid=(DEPTH, n_out // tn),
        in_specs=[pl.BlockSpec((V7X_SUBLANES, D_MODEL), lambda l, j: (0, 0)),
                  pl.BlockSpec((1, D_MODEL, tn), lambda l, j: (l, 0, j)),
                  pl.BlockSpec((1, 1, tn), lambda l, j: (l, 0, j))],
        out_specs=pl.BlockSpec((1, V7X_SUBLANES, tn), lambda l, j: (l, 0, j)),
        out_shape=jax.ShapeDtypeStruct((DEPTH, V7X_SUBLANES, n_out), F32),
        compiler_params=_params(("parallel", "parallel"), 40),
        name="ada_table",
    )(cond8, ada_w, ada_b.reshape(DEPTH, 1, n_out))


def _matmul_cols_kernel(h_ref, w_ref, o_ref):
    o_ref[...] = _dot(h_ref[...], w_ref[0])


def matmul_cols(h, w, li, tn, name):
    m, k = h.shape
    tm = _tile_rows(m)
    n_out = w.shape[2]
    return pl.pallas_call(
        _matmul_cols_kernel,
        grid=(m // tm, n_out // tn),
        in_specs=[pl.BlockSpec((tm, k), lambda i, j: (i, 0)),
                  pl.BlockSpec((1, k, tn), lambda i, j: (li, 0, j))],
        out_specs=pl.BlockSpec((tm, tn), lambda i, j: (i, j)),
        out_shape=jax.ShapeDtypeStruct((m, n_out), F32),
        compiler_params=_params(("parallel", "parallel"), 40),
        name=name,
    )(h, w)


def _normmod_kernel(row, x_ref, g_ref, sh_ref, sc_ref, h_ref):
    h_ref[...] = _normmod(x_ref[...], g_ref, sh_ref, sc_ref, row)


def normmod(x, gains, mods, layer, row, chunk, name):
    m = x.shape[0]
    tm = min(m, 512)
    return pl.pallas_call(
        functools.partial(_normmod_kernel, row),
        grid=(m // tm,),
        in_specs=[pl.BlockSpec((tm, D_MODEL), lambda i: (i, 0)),
                  _gain_spec(layer, D_MODEL, 1),
                  _mod_spec(layer, chunk, 1),
                  _mod_spec(layer, chunk + 1, 1)],
        out_specs=pl.BlockSpec((tm, D_MODEL), lambda i: (i, 0)),
        out_shape=jax.ShapeDtypeStruct((m, D_MODEL), BF16),
        compiler_params=_params(("parallel",), 32),
        name=name,
    )(x, gains, mods, mods)


def _round_weights(pairs):
    @pl.when(pl.program_id(1) == 0)
    def _():
        for w_ref, w_s in pairs:
            w_s[...] = w_ref[0].astype(BF16)


def _wcast_matmul_kernel(h_ref, w_ref, o_ref, w_s):
    _round_weights([(w_ref, w_s)])
    o_ref[...] = _dot(h_ref[...], w_s[...])


def wcast_matmul(h, w, li, name):
    m, k = h.shape
    n_out = w.shape[2]
    tm = _tile_rows(m)
    return pl.pallas_call(
        _wcast_matmul_kernel,
        grid=(1, m // tm),
        in_specs=[pl.BlockSpec((tm, k), lambda j, i: (i, 0)),
                  pl.BlockSpec((1, k, n_out), lambda j, i: (li, 0, 0), pipeline_mode=pl.Buffered(1))],
        out_specs=pl.BlockSpec((tm, n_out), lambda j, i: (i, 0)),
        out_shape=jax.ShapeDtypeStruct((m, n_out), F32),
        scratch_shapes=[pltpu.VMEM((k, n_out), BF16)],
        compiler_params=_params(("parallel", "arbitrary"), 48),
        name=name,
    )(h, w)


def _matmul_residual_kernel(n_lat, with_ctx, a_ref, w_ref, x_ref, gate_ref, g_ref, sh_ref, sc_ref, *rest):
    def rows(a, x, row, o_ref, h_ref):
        out = x[...] + gate_ref[0, row:row + 1, :] * _dot(a[...], w_s[...])
        o_ref[...] = out
        h_ref[...] = _normmod(out, g_ref, sh_ref, sc_ref, row)

    if with_ctx:
        ac_ref, xc_ref, o_ref, h_ref, oc_ref, hc_ref, w_s = rest
    else:
        o_ref, h_ref, w_s = rest
    _round_weights([(w_ref, w_s)])
    rows(a_ref, x_ref, LAT_ROW, o_ref, h_ref)
    if with_ctx:
        pl.when(pl.program_id(1) == n_lat - 1)(lambda: rows(ac_ref, xc_ref, CTX_ROW, oc_ref, hc_ref))


def matmul_residual(lat, ctx, w, li, mods, layer, gains2, name):
    a, x = lat
    m, k = a.shape
    tm = 2 * HEAD_ROWS
    with_ctx = ctx is not None
    row_spec = pl.BlockSpec((tm, D_MODEL), lambda j, i: (i, 0))
    in_specs = [pl.BlockSpec((tm, k), lambda j, i: (i, 0)),
                pl.BlockSpec((1, k, D_MODEL), lambda j, i: (li, 0, 0), pipeline_mode=pl.Buffered(1)),
                row_spec,
                _mod_spec(layer, 2, 2),
                _gain_spec(layer, D_MODEL, 2),
                _mod_spec(layer, 3, 2),
                _mod_spec(layer, 4, 2)]
    out_specs = [row_spec, row_spec]
    out_shape = [jax.ShapeDtypeStruct((m, D_MODEL), F32), jax.ShapeDtypeStruct((m, D_MODEL), BF16)]
    args = [a, w, x, mods, gains2, mods, mods]
    if with_ctx:
        ac, xc = ctx
        mc = xc.shape[0]
        whole = lambda cols: pl.BlockSpec((mc, cols), lambda j, i: (0, 0), pipeline_mode=pl.Buffered(1))
        in_specs += [whole(k), whole(D_MODEL)]
        args += [ac, xc]
        out_specs += [pl.BlockSpec((mc, D_MODEL), lambda j, i: (0, 0))] * 2
        out_shape += [jax.ShapeDtypeStruct((mc, D_MODEL), F32), jax.ShapeDtypeStruct((mc, D_MODEL), BF16)]
    outs = pl.pallas_call(
        functools.partial(_matmul_residual_kernel, m // tm, with_ctx),
        grid=(1, m // tm),
        in_specs=in_specs,
        out_specs=out_specs,
        out_shape=out_shape,
        scratch_shapes=[pltpu.VMEM((k, D_MODEL), BF16)],
        compiler_params=_params(("parallel", "arbitrary"), 60),
        name=name,
    )(*args)
    result = {"lat": (outs[1], outs[0])}
    if with_ctx:
        result["ctx"] = (outs[3], outs[2])
    return result


def _ffn_up_kernel(n_lat, with_ctx, *refs):
    if with_ctx:
        h_ref, wg_ref, wu_ref, wd_ref, hc_ref, a_ref, wdb_ref, ac_ref, wg_s, wu_s = refs
    else:
        h_ref, wg_ref, wu_ref, wd_ref, a_ref, wdb_ref, wg_s, wu_s = refs
    _round_weights([(wg_ref, wg_s), (wu_ref, wu_s), (wd_ref, wdb_ref)])

    def swiglu(src_ref, dst_ref):
        h = src_ref[...]
        g = _dot(h, wg_s[...])
        u = _dot(h, wu_s[...])
        dst_ref[...] = (g * jax.nn.sigmoid(g) * u).astype(dst_ref.dtype)

    swiglu(h_ref, a_ref)
    if with_ctx:
        pl.when(pl.program_id(1) == n_lat - 1)(lambda: swiglu(hc_ref, ac_ref))


def _ffn_down_kernel(n_lat, final_norm, with_ctx, a_ref, wd_ref, x_ref, gate_ref, g_ref, sh_ref, sc_ref, *rest):
    def rows(a, x, row, o_ref, h_ref):
        out = x[...] + gate_ref[0, row:row + 1, :] * _dot(a[...], wd_ref[...])
        if final_norm:
            o_ref[...] = _rms(out, g_ref[0])
        else:
            o_ref[...] = out
            h_ref[...] = _normmod(out, g_ref, sh_ref, sc_ref, row)

    if with_ctx:
        ac_ref, xc_ref, o_ref, h_ref, oc_ref, hc_ref = rest
        rows(a_ref, x_ref, LAT_ROW, o_ref, h_ref)
        pl.when(pl.program_id(0) == n_lat - 1)(lambda: rows(ac_ref, xc_ref, CTX_ROW, oc_ref, hc_ref))
    else:
        rows(a_ref, x_ref, LAT_ROW, rest[0], None if final_norm else rest[1])


def ffn_residual(lat, ctx, mods, layer, wg, wu, wd, next_gains, final_norm, name):
    h, x = lat
    m = x.shape[0]
    tm = _tile_rows(m)
    tf = 512
    with_ctx = ctx is not None
    n_up = m // tm
    in_specs = [pl.BlockSpec((tm, D_MODEL), lambda f, i: (i, 0)),
                pl.BlockSpec((1, D_MODEL, tf), lambda f, i: (layer, 0, f)),
                pl.BlockSpec((1, D_MODEL, tf), lambda f, i: (layer, 0, f)),
                pl.BlockSpec((1, tf, D_MODEL), lambda f, i: (layer, f, 0))]
    out_specs = [pl.BlockSpec((tm, tf), lambda f, i: (i, f)),
                 pl.BlockSpec((tf, D_MODEL), lambda f, i: (f, 0))]
    out_shape = [jax.ShapeDtypeStruct((m, D_FF), BF16), jax.ShapeDtypeStruct((D_FF, D_MODEL), BF16)]
    args = [h, wg, wu, wd]
    if with_ctx:
        hc, xc = ctx
        mc = xc.shape[0]
        in_specs.append(pl.BlockSpec((mc, D_MODEL), lambda f, i: (0, 0)))
        out_specs.append(pl.BlockSpec((mc, tf), lambda f, i: (0, f)))
        out_shape.append(jax.ShapeDtypeStruct((mc, D_FF), BF16))
        args.append(hc)
    acts = pl.pallas_call(
        functools.partial(_ffn_up_kernel, n_up, with_ctx),
        grid=(D_FF // tf, n_up),
        in_specs=in_specs,
        out_specs=out_specs,
        out_shape=out_shape,
        scratch_shapes=[pltpu.VMEM((D_MODEL, tf), BF16), pltpu.VMEM((D_MODEL, tf), BF16)],
        compiler_params=_params(("parallel", "arbitrary"), 56),
        name=name + "_up",
    )(*args)

    tr = HEAD_ROWS
    n_dn = m // tr
    nxt = 0 if final_norm else layer + 1
    row_spec = pl.BlockSpec((tr, D_MODEL), lambda i: (i, 0))
    in_specs = [pl.BlockSpec((tr, D_FF), lambda i: (i, 0)),
                pl.BlockSpec((D_FF, D_MODEL), lambda i: (0, 0), pipeline_mode=pl.Buffered(1)),
                row_spec,
                _mod_spec(layer, 5, 1),
                _gain_spec(nxt, D_MODEL, 1),
                _mod_spec(nxt, 0, 1),
                _mod_spec(nxt, 1, 1)]
    out_specs = [row_spec]
    out_shape = [jax.ShapeDtypeStruct((m, D_MODEL), F32)]
    args = [acts[0], acts[1], x, mods, next_gains, mods, mods]
    if not final_norm:
        out_specs.append(row_spec)
        out_shape.append(jax.ShapeDtypeStruct((m, D_MODEL), BF16))
    if with_ctx:
        whole = lambda cols: pl.BlockSpec((mc, cols), lambda i: (0, 0), pipeline_mode=pl.Buffered(1))
        in_specs += [whole(D_FF), whole(D_MODEL)]
        args += [acts[2], xc]
        out_specs += [pl.BlockSpec((mc, D_MODEL), lambda i: (0, 0))] * 2
        out_shape += [jax.ShapeDtypeStruct((mc, D_MODEL), F32), jax.ShapeDtypeStruct((mc, D_MODEL), BF16)]
    outs = pl.pallas_call(
        functools.partial(_ffn_down_kernel, n_dn, final_norm, with_ctx),
        grid=(n_dn,),
        in_specs=in_specs,
        out_specs=out_specs,
        out_shape=out_shape,
        compiler_params=_params(("arbitrary",), 56),
        name=name + "_down",
    )(*args)
    if final_norm:
        return {"lat": (outs[0], None)}
    state = {"lat": (outs[0], outs[1])}
    if with_ctx:
        state["ctx"] = (outs[2], outs[3])
    return state


def _shift_rows(x, k):
    n = x.shape[0]
    return pltpu.roll(x, (n - k) % n, axis=0)


def _even_mid_kernel(seq_len, zp_ref, z_ref, zn_ref, pw_ref, ps_ref, cw_ref, o_ref):
    i = pl.program_id(0)
    tm = z_ref.shape[0]
    first = i == 0
    last = i == pl.num_programs(0) - 1
    row = i * tm + lax.broadcasted_iota(jnp.int32, (tm, 1), 0)

    def with_halo(lo, hi):
        before = jnp.where(first, 0.0, zp_ref[:, lo:hi])
        after = jnp.where(last, 0.0, zn_ref[:, lo:hi])
        return jnp.concatenate([before, z_ref[:, lo:hi], after], axis=0)

    for g, w in enumerate(POOL_WINDOWS):
        lo_c, hi_c = g * POOL_GROUP, (g + 1) * POOL_GROUP
        u = with_halo(lo_c, hi_c)
        b = u
        s = 1
        while s < w:
            b = b + _shift_rows(b, s)
            s *= 2
        win = _shift_rows(b, -(w // 2))[HALO:HALO + tm]
        cnt = (jnp.minimum(row + (w - w // 2), seq_len) - jnp.maximum(row - w // 2, 0)).astype(F32)
        p = win / cnt - z_ref[:, lo_c:hi_c]
        y = _dot(p.astype(BF16), pw_ref[0, g]) * ps_ref[0, :, lo_c:hi_c]
        o_ref[:, lo_c:hi_c] = y.astype(o_ref.dtype)

    c0 = POOL_WIDTH
    gate_b = z_ref[:, c0:c0 + CONV_WIDTH]
    u = with_halo(c0 + CONV_WIDTH, c0 + 2 * CONV_WIDTH) * with_halo(c0 + 2 * CONV_WIDTH, c0 + 3 * CONV_WIDTH)
    conv = (_shift_rows(u, -1) * cw_ref[0, 0:1, :] + u * cw_ref[0, 1:2, :] + _shift_rows(u, 1) * cw_ref[0, 2:3, :])
    y_b = gate_b * conv[HALO:HALO + tm]
    o_ref[:, POOL_WIDTH:] = y_b.astype(o_ref.dtype)


def even_mid(z, pool_w, pool_scale, conv_w, li, name):
    m = z.shape[0]
    tm = min(m, 512)
    nb = tm // HALO
    last_blk = m // HALO - 1
    n_win = len(POOL_WINDOWS)
    return pl.pallas_call(
        functools.partial(_even_mid_kernel, m),
        grid=(m // tm,),
        in_specs=[pl.BlockSpec((HALO, EVEN_IN), lambda i: (jnp.maximum(i * nb - 1, 0), 0)),
                  pl.BlockSpec((tm, EVEN_IN), lambda i: (i, 0)),
                  pl.BlockSpec((HALO, EVEN_IN), lambda i: (jnp.minimum((i + 1) * nb, last_blk), 0)),
                  pl.BlockSpec((1, n_win, POOL_GROUP, POOL_GROUP), lambda i: (li, 0, 0, 0)),
                  pl.BlockSpec((1, 1, POOL_WIDTH), lambda i: (li, 0, 0)),
                  pl.BlockSpec((1, 3, CONV_WIDTH), lambda i: (li, 0, 0))],
        out_specs=pl.BlockSpec((tm, POOL_WIDTH + CONV_WIDTH), lambda i: (i, 0)),
        out_shape=jax.ShapeDtypeStruct((m, POOL_WIDTH + CONV_WIDTH), BF16),
        compiler_params=_params(("parallel",), 48),
        name=name,
    )(z, z, z, pool_w, pool_scale, conv_w)


def _rope_lanes(t, cos_ref, sin_ref):
    return t * cos_ref[...] + pltpu.roll(t, QK_ROPE, axis=1) * sin_ref[...]


def _q_up_kernel(scale, a_ref, g_ref, wt_ref, cos_t_ref, sin_t_ref, qt_ref):
    n_t = _rms(a_ref[...], g_ref[0]).T.astype(BF16)
    cos_t, sin_t = cos_t_ref[...] * scale, sin_t_ref[...] * scale
    rope_end = QK_NOPE + QK_ROPE
    zeros = jnp.zeros((HEAD_PAD - rope_end, n_t.shape[1]), qt_ref.dtype)
    for h in range(MLA_HEADS):
        q_t = _dot(wt_ref[0, h * HEAD_PAD:(h + 1) * HEAD_PAD, :], n_t)
        qt_ref[h, :QK_NOPE, :] = (q_t[:QK_NOPE] * scale).astype(qt_ref.dtype)
        rope = q_t[QK_NOPE:rope_end] * cos_t + q_t[rope_end:] * sin_t
        qt_ref[h, QK_NOPE:rope_end, :] = rope.astype(qt_ref.dtype)
        qt_ref[h, rope_end:, :] = zeros


def q_up(qkv_a, gains, li, w_uq_t, cos_t, sin_t, name):
    m = qkv_a.shape[0]
    tr = HEAD_ROWS
    scale = float((QK_NOPE + QK_ROPE) ** -0.5 * np.log2(np.e))
    return pl.pallas_call(
        functools.partial(_q_up_kernel, scale),
        grid=(m // tr,),
        in_specs=[pl.BlockSpec((tr, Q_LORA), lambda i: (i, 0)),
                  _gain_spec(li, Q_LORA, 1),
                  pl.BlockSpec((1, MLA_HEADS * HEAD_PAD, Q_LORA), lambda i: (li, 0, 0)),
                  pl.BlockSpec((QK_ROPE, tr), lambda i: (0, i)),
                  pl.BlockSpec((QK_ROPE, tr), lambda i: (0, i))],
        out_specs=pl.BlockSpec((MLA_HEADS, HEAD_PAD, tr), lambda i: (0, 0, i)),
        out_shape=jax.ShapeDtypeStruct((MLA_HEADS, HEAD_PAD, m), BF16),
        compiler_params=_params(("parallel",), 40),
        name=name,
    )(qkv_a, gains, w_uq_t, cos_t, sin_t)


def _kv_up_kernel(c_ref, r_ref, g_ref, w_ref, cos_ref, sin_ref, k_ref, vt_ref):
    n = _rms(c_ref[...], g_ref[0]).astype(BF16)
    k_rope = _rope_lanes(r_ref[...], cos_ref, sin_ref).astype(k_ref.dtype)
    kv = _dot(n, w_ref[0])
    ones = jnp.ones((VT_ROWS - V_DIM, vt_ref.shape[2]), vt_ref.dtype)
    width = QK_NOPE + V_DIM
    for h in range(MLA_HEADS):
        base = h * width
        k_ref[h, :, :QK_NOPE] = kv[:, base:base + QK_NOPE].astype(k_ref.dtype)
        k_ref[h, :, QK_NOPE:] = k_rope
        vt_ref[h, :V_DIM, :] = kv[:, base + QK_NOPE:base + width].T.astype(vt_ref.dtype)
        vt_ref[h, V_DIM:, :] = ones


def kv_up(qkv_a, gains, li, w_ukv, cos, sin, name):
    m = qkv_a.shape[0]
    tr = HEAD_ROWS
    rope_blk = (Q_LORA + KV_LORA) // V7X_LANES
    return pl.pallas_call(
        _kv_up_kernel,
        grid=(m // tr,),
        in_specs=[pl.BlockSpec((tr, KV_LORA), lambda i: (i, 1)),
                  pl.BlockSpec((tr, V7X_LANES), lambda i: (i, rope_blk)),
                  _gain_spec(li, KV_LORA, 1),
                  pl.BlockSpec((1, KV_LORA, MLA_HEADS * (QK_NOPE + V_DIM)), lambda i: (li, 0, 0)),
                  pl.BlockSpec((tr, V7X_LANES), lambda i: (i, 0)),
                  pl.BlockSpec((tr, V7X_LANES), lambda i: (i, 0))],
        out_specs=[pl.BlockSpec((MLA_HEADS, tr, HEAD_PAD), lambda i: (0, i, 0)),
                   pl.BlockSpec((MLA_HEADS, VT_ROWS, tr), lambda i: (0, 0, i))],
        out_shape=[jax.ShapeDtypeStruct((MLA_HEADS, m, HEAD_PAD), BF16),
                   jax.ShapeDtypeStruct((MLA_HEADS, VT_ROWS, m), BF16)],
        compiler_params=_params(("parallel",), 40),
        name=name,
    )(qkv_a, qkv_a, gains, w_ukv, cos, sin)


def _flash_kernel(piece, n_lat, has_ctx, *refs):
    if has_ctx:
        qt_ref, k_ref, vt_ref, kc_ref, vtc_ref, o_ref = refs
    else:
        qt_ref, k_ref, vt_ref, o_ref = refs
    qt = qt_ref[0]
    tq = qt.shape[1]
    n_all = n_lat + int(has_ctx)

    def keys(i):
        return kc_ref[0] if i >= n_lat else k_ref[0, i * piece:(i + 1) * piece, :]

    def values_t(i):
        return vtc_ref[0] if i >= n_lat else vt_ref[0, :, i * piece:(i + 1) * piece]

    m = jnp.full((1, tq), -jnp.inf, F32)
    acc = jnp.zeros((VT_ROWS, tq), F32)
    pending = [_dot(keys(i), qt) for i in range(min(ATTN_AHEAD, n_all))]
    for t in range(n_all):
        s = pending.pop(0)
        if t + ATTN_AHEAD < n_all:
            pending.append(_dot(keys(t + ATTN_AHEAD), qt))
        m_new = jnp.maximum(m, jnp.max(s, axis=0, keepdims=True))
        p = jnp.exp2(s - m_new).astype(BF16)
        acc = jnp.exp2(m - m_new) * acc + _dot(values_t(t), p)
        m = m_new
    o_ref[...] = (acc[:V_DIM] / acc[V_DIM:V_DIM + 1]).T.astype(o_ref.dtype)


def flash_attention(qt, k, vt, k_ctx, vt_ctx, name):
    h, _, m = qt.shape
    n_k = k.shape[1]
    tq = min(m, ATTN_TQ)
    piece = min(n_k, ATTN_PIECE)
    has_ctx = k_ctx is not None
    in_specs = [pl.BlockSpec((1, HEAD_PAD, tq), lambda hh, i: (hh, 0, i)),
                pl.BlockSpec((1, n_k, HEAD_PAD), lambda hh, i: (hh, 0, 0)),
                pl.BlockSpec((1, VT_ROWS, n_k), lambda hh, i: (hh, 0, 0))]
    args = [qt, k, vt]
    if has_ctx:
        n_c = k_ctx.shape[1]
        in_specs += [pl.BlockSpec((1, n_c, HEAD_PAD), lambda hh, i: (hh, 0, 0)),
                     pl.BlockSpec((1, VT_ROWS, n_c), lambda hh, i: (hh, 0, 0))]
        args += [k_ctx, vt_ctx]
    return pl.pallas_call(
        functools.partial(_flash_kernel, piece, n_k // piece, has_ctx),
        grid=(h, m // tq),
        in_specs=in_specs,
        out_specs=pl.BlockSpec((tq, V_DIM), lambda hh, i: (i, hh)),
        out_shape=jax.ShapeDtypeStruct((m, h * V_DIM), BF16),
        compiler_params=_params(("parallel", "parallel"), 48),
        name=name,
    )(*args)


def _swap_pairs(w):
    q = QK_ROPE // 4
    r1, r2, c1, c2 = (w[..., j * q:(j + 1) * q] for j in range(4))
    return jnp.concatenate([-r2, r1, -c2, c1], axis=-1)


def _rope_tables(n):
    q = QK_ROPE // 4
    rows = n // GRID_W
    inv = jnp.power(jnp.float32(ROPE_THETA), -jnp.arange(q, dtype=F32) / q)
    ar = jnp.arange(rows).astype(F32)[:, None] * inv
    ac = jnp.arange(GRID_W).astype(F32)[:, None] * inv
    cos_r, sin_r = (jnp.repeat(f(ar), GRID_W, axis=0) for f in (jnp.cos, jnp.sin))
    cos_c, sin_c = (jnp.tile(f(ac), (rows, 1)) for f in (jnp.cos, jnp.sin))
    zeros = jnp.zeros((n, V7X_LANES - QK_ROPE), F32)
    cos = jnp.concatenate([cos_r, cos_r, cos_c, cos_c, zeros], axis=1)
    sin = jnp.concatenate([sin_r, sin_r, sin_c, sin_c, zeros], axis=1)
    return cos, sin


def _identity_tables(n):
    cos = jnp.concatenate([jnp.ones((n, QK_ROPE), F32), jnp.zeros((n, V7X_LANES - QK_ROPE), F32)], axis=1)
    return cos, jnp.zeros((n, V7X_LANES), F32)


def _mla_weights(w_dq, w_uq, w_dkv):
    n = w_dq.shape[0]
    rope = w_dkv[..., KV_LORA:]
    w_down = jnp.concatenate([w_dq, w_dkv[..., :KV_LORA], rope, _swap_pairs(rope)], axis=-1)
    wq = w_uq.reshape(n, Q_LORA, MLA_HEADS, QK_NOPE + QK_ROPE)
    w_up_q = jnp.concatenate([wq, _swap_pairs(wq[..., QK_NOPE:])], axis=-1)
    w_up_q_t = w_up_q.reshape(n, Q_LORA, MLA_HEADS * HEAD_PAD).transpose(0, 2, 1)
    return w_down, w_up_q_t.astype(BF16)


def kernel(x, c, ctx, c_ctx, ada_w, ada_b, norm1_g, norm2_g, even_w_in, pool_w, pool_scale, conv_w,
           even_w_out, mla_w_dq, mla_q_norm_g, mla_w_uq, mla_w_dkv, mla_kv_norm_g, mla_w_ukv, mla_w_o,
           ffn_w_gate, ffn_w_up, ffn_w_down, final_norm_g):
    n_lat, n_ctx = x.shape[1], ctx.shape[1]
    x_lat, x_ctx = x[0], ctx[0]

    cond8 = jnp.concatenate([c, c_ctx[None, :], jnp.zeros((V7X_SUBLANES - 2, D_MODEL), F32)], axis=0)
    mods = ada_table(cond8, ada_w, ada_b)
    g1 = norm1_g.reshape(DEPTH, 1, D_MODEL)
    g2 = norm2_g.reshape(DEPTH, 1, D_MODEL)
    gq = mla_q_norm_g.reshape(-1, 1, Q_LORA)
    gkv = mla_kv_norm_g.reshape(-1, 1, KV_LORA)
    final_g = final_norm_g.reshape(1, 1, D_MODEL)
    rope_lat = _rope_tables(n_lat)
    rope_ctx = _identity_tables(n_ctx)

    w_in, pw = even_w_in.astype(BF16), pool_w.astype(BF16)
    ps = pool_scale.reshape(-1, 1, POOL_WIDTH)
    rope_t = {"lat": tuple(t[:, :QK_ROPE].T for t in rope_lat), "ctx": tuple(t[:, :QK_ROPE].T for t in rope_ctx)}
    w_down, w_up_q_t = _mla_weights(mla_w_dq, mla_w_uq, mla_w_dkv)
    w_ukv = mla_w_ukv.astype(BF16)

    state = {"lat": (x_lat, normmod(x_lat, g1, mods, 0, LAT_ROW, 0, "norm_in_lat")),
             "ctx": (x_ctx, normmod(x_ctx, g1, mods, 0, CTX_ROW, 0, "norm_in_ctx"))}

    for layer in range(DEPTH):
        last = layer == DEPTH - 1
        odd = layer % 2 == 1
        li = layer // 2
        tags = ["lat"] + (["ctx"] if odd or not last else [])
        branch = {}

        if odd:
            proj = {}
            for tag in tags:
                cos, sin = rope_lat if tag == "lat" else rope_ctx
                a = wcast_matmul(state[tag][1], w_down, li, f"mla_down_{tag}")
                k, vt = kv_up(a, gkv, li, w_ukv, cos, sin, f"kv_up_{tag}")
                need_q = tag == "lat" or not last
                qt = q_up(a, gq, li, w_up_q_t, *rope_t[tag], f"q_up_{tag}") if need_q else None
                proj[tag] = (qt, k, vt)
            qt, k, vt = proj["lat"]
            qtc, kc, vtc = proj["ctx"]
            branch["lat"] = flash_attention(qt, k, vt, kc, vtc, "attn_lat")
            if not last:
                branch["ctx"] = flash_attention(qtc, kc, vtc, None, None, "attn_ctx")
            w_proj, proj_name = mla_w_o, "attn_out"
        else:
            for tag in tags:
                z = matmul_cols(state[tag][1], w_in, li, 512, f"even_in_{tag}")
                branch[tag] = even_mid(z, pw, ps, conv_w, li, f"even_mid_{tag}")
            w_proj, proj_name = even_w_out, "even_out"

        proj_in = {tag: (y, state[tag][0]) for tag, y in branch.items()}
        ffn_in = matmul_residual(proj_in["lat"], proj_in.get("ctx"), w_proj, li, mods, layer, g2, proj_name)
        state.update(ffn_residual(ffn_in["lat"], ffn_in.get("ctx"), mods, layer, ffn_w_gate, ffn_w_up, ffn_w_down,
                                  final_g if last else g1, last, "ffn"))

    return state["lat"][0][None]
```
